```python
import math
import jax, jax.numpy as jnp
from jax import lax
import numpy as np

D_MODEL = 4096
BATCH = 4
SEQ = 2048
DEPTH = 1
DEC_BATCH = 128
DEC_SEQ = 4
PAST_LEN = 2048
PAGE_SIZE = 128

HEAD_DIM = 128
HA_HEADS = D_MODEL // (2 * HEAD_DIM)
HA_K = HEAD_DIM
HA_V = HEAD_DIM
HGRN_CHUNK = 64
HB_HEADS = D_MODEL // (2 * HEAD_DIM)
KV_HEADS = HB_HEADS // 4
ROT_DIM = HEAD_DIM // 4
ROPE_THETA = 500000.0
IDX_HEADS = 32
IDX_DIM = 128
IDX_ROT_DIM = 64
TOPK_MAX = 256
Q_BLOCK = 128
MIX_WIDTH = HA_HEADS * HA_V + HB_HEADS * HEAD_DIM
N_EXPERTS = 32
TOP_K = 4
D_FF = D_MODEL
SWIGLU_LIMIT = 7.0
SWIGLU_ALPHA = 1.702
NORM_EPS = 1e-6
F32 = jnp.float32

kernel_name = 'hymba_hgrn2_dsa_moe_step'


def _in_sizes():
    return [HA_HEADS * HA_K, HA_HEADS * HA_K, HA_HEADS * HA_V, HA_HEADS * HA_V,
            HB_HEADS * HEAD_DIM, KV_HEADS * HEAD_DIM, KV_HEADS * HEAD_DIM,
            IDX_HEADS * IDX_DIM, IDX_DIM, IDX_HEADS]


def _split_in(u):
    offs = np.cumsum(_in_sizes())[:-1].tolist()
    return jnp.split(u, offs, axis=-1)


def rmsnorm(x, g):
    xf = x.astype(F32)
    y = xf * lax.rsqrt(jnp.mean(xf * xf, axis=-1, keepdims=True) + NORM_EPS)
    return (y * g.astype(F32)).astype(x.dtype)


def rope_partial(x, pos, rot):
    half = rot // 2
    inv = ROPE_THETA ** (-jnp.arange(half, dtype=F32) * 2.0 / rot)
    ang = pos.astype(F32)[:, None] * inv[None, :]
    cos = jnp.cos(ang)[:, None, :]
    sin = jnp.sin(ang)[:, None, :]
    xf = x.astype(F32)
    x1, x2, rest = xf[..., :half], xf[..., half:rot], xf[..., rot:]
    out = jnp.concatenate([x1 * cos - x2 * sin, x2 * cos + x1 * sin, rest], axis=-1)
    return out.astype(x.dtype)


def ada_mod(c, w, b):
    m = jax.nn.silu(c) @ w + b
    return jnp.split(m[:, None, :], 6, axis=-1)


def hgrn2_chunked(q, k, v, logf, s0, chunk):
    N, L, H, K = q.shape
    V = v.shape[-1]
    n = L // chunk

    def blocks(a):
        return a.astype(F32).reshape(N, n, chunk, H, a.shape[-1]).transpose(1, 0, 3, 2, 4)

    causal = jnp.tril(jnp.ones((chunk, chunk), dtype=bool))[:, :, None]

    def step(S, inp):
        qc, kc, vc, gc = inp
        G = jnp.cumsum(gc, axis=2)
        diff = G[:, :, :, None, :] - G[:, :, None, :, :]
        dec = jnp.exp(jnp.where(causal, diff, -jnp.inf))
        A = jnp.einsum('nhtk,nhsk,nhtsk->nhts', qc, kc, dec)
        o = jnp.einsum('nhts,nhsv->nhtv', A, vc) + jnp.einsum('nhtk,nhkv->nhtv', qc * jnp.exp(G), S)
        G_end = G[:, :, -1:, :]
        S = jnp.exp(G_end[:, :, 0, :])[..., None] * S + jnp.einsum('nhsk,nhsv->nhkv', kc * jnp.exp(G_end - G), vc)
        return S, o

    S, o = lax.scan(step, s0.astype(F32), (blocks(q), blocks(k), blocks(v), blocks(logf)))
    o = o.transpose(1, 0, 3, 2, 4).reshape(N, L, H, V)
    return o.astype(v.dtype), S.astype(s0.dtype)


def sparse_attend(q, qi, wi, qpos, k, v, ki, kpos, n_sel):
    N, Q = q.shape[:2]
    dots = jnp.einsum('nqhd,nsd->nqhs', qi.astype(F32), ki.astype(F32))
    score = jnp.einsum('nqh,nqhs->nqs', wi.astype(F32), jax.nn.relu(dots)) * IDX_DIM ** -0.5
    causal = kpos[None, None, :] <= qpos[None, :, None]
    score = jnp.where(causal, score, -jnp.inf)
    top_val, top_idx = lax.top_k(score, n_sel)
    valid = top_val > -jnp.inf
    take = jax.vmap(lambda a, i: a[i])
    kg = take(k, top_idx)
    vg = take(v, top_idx)
    qg = q.reshape(N, Q, KV_HEADS, HB_HEADS // KV_HEADS, HEAD_DIM)
    logits = jnp.einsum('nqgrd,nqkgd->nqgrk', qg.astype(F32), kg.astype(F32)) * HEAD_DIM ** -0.5
    logits = jnp.where(valid[:, :, None, None, :], logits, -jnp.inf)
    p = jax.nn.softmax(logits, axis=-1)
    o = jnp.einsum('nqgrk,nqkgd->nqgrd', p.astype(vg.dtype), vg)
    return o.reshape(N, Q, HB_HEADS, HEAD_DIM)


def dsa_prompt(q, k, v, qi, ki, wi, pos):
    N, L = q.shape[:2]
    n_sel = min(TOPK_MAX, L // 4)
    qb = min(Q_BLOCK, L)
    nb = L // qb

    def one_block(i):
        start = i * qb
        sl = lambda a: lax.dynamic_slice_in_dim(a, start, qb, axis=1)
        return sparse_attend(sl(q), sl(qi), sl(wi), lax.dynamic_slice_in_dim(pos, start, qb),
                             k, v, ki, pos, n_sel)

    out = lax.map(one_block, jnp.arange(nb))
    return jnp.moveaxis(out, 0, 1).reshape(N, L, HB_HEADS, HEAD_DIM)


def gather_pages(pool, page_table):
    g = pool[page_table]
    return g.reshape(page_table.shape[0], page_table.shape[1] * pool.shape[1], *pool.shape[2:])


def token_mixers(h, pos, s0, past, lb, w_in, w_out, g_q, g_k, g_kidx, g_o, chunk):
    N, L, _ = h.shape
    u = h @ w_in
    qa, fa, ia, ga, qb, kb, vb, qi, ki, wi = _split_in(u)
    f = lb + (1.0 - lb) * jax.nn.sigmoid(fa.astype(F32))
    logf = jnp.log(f).reshape(N, L, HA_HEADS, HA_K)
    kin = (1.0 - f).reshape(N, L, HA_HEADS, HA_K)
    qa = qa.reshape(N, L, HA_HEADS, HA_K) * HA_K ** -0.5
    oa, s_new = hgrn2_chunked(qa, kin, ia.reshape(N, L, HA_HEADS, HA_V), logf, s0, chunk)
    oa = rmsnorm(oa, g_o).reshape(N, L, HA_HEADS * HA_V) * jax.nn.silu(ga)
    qb = rope_partial(rmsnorm(qb.reshape(N, L, HB_HEADS, HEAD_DIM), g_q), pos, ROT_DIM)
    kb = rope_partial(rmsnorm(kb.reshape(N, L, KV_HEADS, HEAD_DIM), g_k), pos, ROT_DIM)
    vb = vb.reshape(N, L, KV_HEADS, HEAD_DIM)
    qi = rope_partial(qi.reshape(N, L, IDX_HEADS, IDX_DIM), pos, IDX_ROT_DIM)
    ki = rope_partial(rmsnorm(ki, g_kidx)[:, :, None, :], pos, IDX_ROT_DIM)[:, :, 0, :]
    wi = wi * IDX_HEADS ** -0.5
    if past is None:
        ob = dsa_prompt(qb, kb, vb, qi, ki, wi, pos)
    else:
        k_all = jnp.concatenate([past[0], kb], axis=1)
        v_all = jnp.concatenate([past[1], vb], axis=1)
        ki_all = jnp.concatenate([past[2], ki], axis=1)
        n_keys = k_all.shape[1]
        ob = sparse_attend(qb, qi, wi, pos, k_all, v_all, ki_all, jnp.arange(n_keys), min(TOPK_MAX, n_keys // 4))
    o = jnp.concatenate([oa, ob.reshape(N, L, HB_HEADS * HEAD_DIM)], axis=-1) @ w_out
    return o, kb, vb, ki, s_new


def moe(h, w_router, b_router, w1, b1, w2, b2):
    logits = (h @ w_router + b_router).astype(F32)
    top_val, top_idx = lax.top_k(logits, TOP_K)
    gates = jax.nn.softmax(top_val, axis=-1)
    gate_e = jnp.sum(jax.nn.one_hot(top_idx, N_EXPERTS, dtype=F32) * gates[..., None], axis=1)

    def expert(acc, p):
        w1e, b1e, w2e, b2e, ge = p
        hh = h @ w1e + b1e
        x_glu = jnp.minimum(hh[:, 0::2], SWIGLU_LIMIT)
        x_lin = jnp.clip(hh[:, 1::2], -SWIGLU_LIMIT, SWIGLU_LIMIT)
        a = x_glu * jax.nn.sigmoid(SWIGLU_ALPHA * x_glu) * (x_lin + 1.0)
        y = a @ w2e + b2e
        return acc + ge[:, None] * y.astype(F32), None

    acc, _ = lax.scan(expert, jnp.zeros(h.shape, F32), (w1, b1, w2, b2, gate_e.T))
    return acc.astype(h.dtype)


def setup_inputs(seed: int = 0) -> dict:
    key = jax.random.key(seed)
    ks = jax.random.split(key, 32)
    n_pages = PAST_LEN // PAGE_SIZE
    n_used = DEC_BATCH * n_pages
    n_pool = n_used + max(1, n_used // 4)
    in_cols = sum(_in_sizes())
    nrm = lambda k, shape, s: jax.random.normal(k, shape, F32) * s
    gain = lambda k, shape: 1.0 + 0.05 * jax.random.normal(k, shape, F32)
    page_table = jax.random.permutation(ks[0], n_pool)[:n_used].reshape(DEC_BATCH, n_pages).astype(jnp.int32)
    return {
        'x_prompt': nrm(ks[1], (BATCH, SEQ, D_MODEL), 1.0),
        'x_sample': nrm(ks[2], (DEC_BATCH, DEC_SEQ, D_MODEL), 1.0),
        'cache_k': nrm(ks[3], (DEPTH, n_pool, PAGE_SIZE, KV_HEADS, HEAD_DIM), 1.0),
        'cache_v': nrm(ks[4], (DEPTH, n_pool, PAGE_SIZE, KV_HEADS, HEAD_DIM), 1.0),
        'cache_kidx': nrm(ks[5], (DEPTH, n_pool, PAGE_SIZE, IDX_DIM), 1.0),
        'state_hgrn': nrm(ks[6], (DEPTH, DEC_BATCH, HA_HEADS, HA_K, HA_V), 0.5),
        'page_table': page_table,
        'c_prompt': nrm(ks[7], (BATCH, D_MODEL), 1.0),
        'c_sample': nrm(ks[8], (DEC_BATCH, D_MODEL), 1.0),
        'w_ada': nrm(ks[9], (DEPTH, D_MODEL, 6 * D_MODEL), 0.5 * D_MODEL ** -0.5),
        'b_ada': nrm(ks[10], (DEPTH, 6 * D_MODEL), 0.01),
        'norm1': gain(ks[11], (DEPTH, D_MODEL)),
        'norm2': gain(ks[12], (DEPTH, D_MODEL)),
        'w_in': nrm(ks[13], (DEPTH, D_MODEL, in_cols), D_MODEL ** -0.5),
        'w_out': nrm(ks[14], (DEPTH, MIX_WIDTH, D_MODEL), MIX_WIDTH ** -0.5),
        'hgrn_lb': nrm(ks[15], (DEPTH + 1, HA_HEADS * HA_K), 0.5),
        'g_hgrn_o': gain(ks[16], (DEPTH, HA_V)),
        'g_q': gain(ks[17], (DEPTH, HEAD_DIM)),
        'g_k': gain(ks[18], (DEPTH, HEAD_DIM)),
        'g_kidx': gain(ks[19], (DEPTH, IDX_DIM)),
        'w_router': nrm(ks[20], (DEPTH, D_MODEL, N_EXPERTS), D_MODEL ** -0.5),
        'b_router': nrm(ks[21], (DEPTH, N_EXPERTS), 0.01),
        'w1': nrm(ks[22], (DEPTH, N_EXPERTS, D_MODEL, 2 * D_FF), D_MODEL ** -0.5),
        'b1': nrm(ks[23], (DEPTH, N_EXPERTS, 2 * D_FF), 0.01),
        'w2': nrm(ks[24], (DEPTH, N_EXPERTS, D_FF, D_MODEL), D_FF ** -0.5),
        'b2': nrm(ks[25], (DEPTH, N_EXPERTS, D_MODEL), 0.01),
    }


def reference(x_prompt, x_sample, cache_k, cache_v, cache_kidx, state_hgrn, page_table, c_prompt, c_sample,
              w_ada, b_ada, norm1, norm2, w_in, w_out, hgrn_lb, g_hgrn_o, g_q, g_k, g_kidx,
              w_router, b_router, w1, b1, w2, b2):
    B, S, D = x_prompt.shape
    DB, T, _ = x_sample.shape
    past_len = page_table.shape[1] * cache_k.shape[2]
    pos_p = jnp.arange(S)
    pos_s = past_len + jnp.arange(T)
    chunk_p = math.gcd(S, HGRN_CHUNK)
    chunk_s = math.gcd(T, HGRN_CHUNK)
    lb_cum = jnp.cumsum(jax.nn.softmax(hgrn_lb.astype(F32), axis=0), axis=0)
    xp, xs = x_prompt, x_sample
    kp_l, vp_l, kip_l, sp_l, ks_l, vs_l, kis_l, ss_l = [], [], [], [], [], [], [], []
    for l in range(DEPTH):
        lb = lb_cum[l + 1] - lb_cum[0]
        sh1p, sc1p, g1p, sh2p, sc2p, g2p = ada_mod(c_prompt, w_ada[l], b_ada[l])
        sh1s, sc1s, g1s, sh2s, sc2s, g2s = ada_mod(c_sample, w_ada[l], b_ada[l])
        hp = rmsnorm(xp, norm1[l]) * (1.0 + sc1p) + sh1p
        hs = rmsnorm(xs, norm1[l]) * (1.0 + sc1s) + sh1s
        s0p = jnp.zeros((B, HA_HEADS, HA_K, HA_V), xp.dtype)
        op, kp, vp, kip, sp = token_mixers(hp, pos_p, s0p, None, lb, w_in[l], w_out[l],
                                           g_q[l], g_k[l], g_kidx[l], g_hgrn_o[l], chunk_p)
        past = (gather_pages(cache_k[l], page_table), gather_pages(cache_v[l], page_table),
                gather_pages(cache_kidx[l], page_table))
        os_, ks_, vs_, kis, ss = token_mixers(hs, pos_s, state_hgrn[l], past, lb, w_in[l], w_out[l],
                                              g_q[l], g_k[l], g_kidx[l], g_hgrn_o[l], chunk_s)
        xp = xp + g1p * op
        xs = xs + g1s * os_
        h2p = rmsnorm(xp, norm2[l]) * (1.0 + sc2p) + sh2p
        h2s = rmsnorm(xs, norm2[l]) * (1.0 + sc2s) + sh2s
        m = moe(jnp.concatenate([h2p.reshape(B * S, D), h2s.reshape(DB * T, D)], axis=0),
                w_router[l], b_router[l], w1[l], b1[l], w2[l], b2[l])
        xp = xp + g2p * m[:B * S].reshape(B, S, D)
        xs = xs + g2s * m[B * S:].reshape(DB, T, D)
        kp_l.append(kp); vp_l.append(vp); kip_l.append(kip); sp_l.append(sp)
        ks_l.append(ks_); vs_l.append(vs_); kis_l.append(kis); ss_l.append(ss)
    return (xp, xs, jnp.stack(kp_l), jnp.stack(vp_l), jnp.stack(kip_l), jnp.stack(sp_l),
            jnp.stack(ks_l), jnp.stack(vs_l), jnp.stack(kis_l), jnp.stack(ss_l))
```

```python
import functools
import math

import numpy as np
import jax
import jax.numpy as jnp
from jax import lax
from jax.experimental import pallas as pl
from jax.experimental.pallas import tpu as pltpu

F32 = jnp.float32
BF16 = jnp.bfloat16
I32 = jnp.int32

HEAD_DIM = 128
ROT_DIM = HEAD_DIM // 4
ROPE_THETA = 500000.0
IDX_HEADS = 32
IDX_DIM = 128
IDX_ROT_DIM = 64
TOPK_MAX = 256
TOP_K = 4
SWIGLU_LIMIT = 7.0
SWIGLU_ALPHA = 1.702
NORM_EPS = 1e-6
DEC_PAD = 8

LANES = 128
SUBLANES = 8
VMEM_LIMIT_BYTES = 56 * 1024 * 1024

NEG_BIG = -1e30
MOE_SUB = 256
MOE_SUBS_PER_SUPER = 6


def _cparams(sem):
    return pltpu.CompilerParams(dimension_semantics=sem, vmem_limit_bytes=VMEM_LIMIT_BYTES)


def _dot(a, b, precision=None):
    return jnp.dot(a, b, preferred_element_type=F32, precision=precision)


def _dot_nt(a, b):
    return lax.dot_general(a, b, (((1,), (1,)), ((), ())), preferred_element_type=F32)


def _dot_tn(a, b):
    return lax.dot_general(a, b, (((0,), (0,)), ((), ())), preferred_element_type=F32)


def _sigmoid(x):
    return 1.0 / (1.0 + jnp.exp(-x))


def _pick(n, pref, align):
    if n <= pref:
        return n
    t = (pref // align) * align
    while t >= align:
        if n % t == 0:
            return t
        t -= align
    return n


def _ada_kernel(c_ref, w_ref, b_ref, o_ref):
    c = c_ref[...]
    s = c * _sigmoid(c)
    o_ref[...] = _dot(s.astype(BF16), w_ref[...].astype(BF16)) + b_ref[...]


def _ada(c, w, b):
    m, d = c.shape
    n = w.shape[1]
    tn = _pick(n, 512, LANES)
    return pl.pallas_call(
        _ada_kernel,
        grid=(n // tn,),
        in_specs=[pl.BlockSpec((m, d), lambda j: (0, 0)),
                  pl.BlockSpec((d, tn), lambda j: (0, j)),
                  pl.BlockSpec((1, tn), lambda j: (0, j))],
        out_specs=pl.BlockSpec((m, tn), lambda j: (0, j)),
        out_shape=jax.ShapeDtypeStruct((m, n), F32),
        compiler_params=_cparams(("arbitrary",)),
        name="ada",
    )(c, w, b.reshape(1, n))


def _rms_mod(x, g, sc, sh):
    y = x * lax.rsqrt(jnp.mean(x * x, axis=-1, keepdims=True) + NORM_EPS)
    return (y * g) * (1.0 + sc) + sh


def _norm_mod_kernel(x_ref, g_ref, sc_ref, sh_ref, h_ref):
    h_ref[...] = _rms_mod(x_ref[...], g_ref[...], sc_ref[...], sh_ref[...]).astype(BF16)


def _norm_mod(x, g, sc, sh, bb, tl):
    nb, l, d = x.shape
    xs = pl.BlockSpec((bb, tl, d), lambda b, i: (b, i, 0))
    ms = pl.BlockSpec((bb, 1, d), lambda b, i: (b, 0, 0))
    return pl.pallas_call(
        _norm_mod_kernel,
        grid=(nb // bb, l // tl),
        in_specs=[xs, pl.BlockSpec((1, 1, d), lambda b, i: (0, 0, 0)), ms, ms],
        out_specs=xs,
        out_shape=jax.ShapeDtypeStruct((nb, l, d), BF16),
        compiler_params=_cparams(("arbitrary", "arbitrary")),
        name="norm_mod",
    )(x, g.reshape(1, 1, d), sc, sh)


def _proj_kernel(*refs, n_a):
    a_refs, w_refs, o_ref = refs[:n_a], refs[n_a:2 * n_a], refs[2 * n_a]
    acc = _dot(a_refs[0][...], w_refs[0][...].astype(BF16))
    for i in range(1, n_a):
        acc = acc + _dot(a_refs[i][...], w_refs[i][...].astype(BF16))
    o_ref[...] = acc


def _proj(a_list, w, n_cols, tm_pref=1024, tn_pref=512):
    n_a = len(a_list)
    m = a_list[0].shape[0]
    kk = a_list[0].shape[1]
    assert all(a.shape == (m, kk) for a in a_list) and w.shape[0] == n_a * kk
    tm = _pick(m, tm_pref, 16)
    tn = _pick(n_cols, tn_pref, LANES)
    in_specs = [pl.BlockSpec((tm, kk), lambda i, j: (i, 0)) for _ in range(n_a)]
    in_specs += [pl.BlockSpec((kk, tn), functools.partial(lambda i, j, r: (r, j), r=r)) for r in range(n_a)]
    return pl.pallas_call(
        functools.partial(_proj_kernel, n_a=n_a),
        grid=(m // tm, n_cols // tn),
        in_specs=in_specs,
        out_specs=pl.BlockSpec((tm, tn), lambda i, j: (i, j)),
        out_shape=jax.ShapeDtypeStruct((m, n_cols), F32),
        compiler_params=_cparams(("arbitrary", "arbitrary")),
        name="proj",
    )(*a_list, *([w] * n_a))


HGRN_SUB = 16


def _hgrn_chunk(qc, fc, vc, lb, st, consts, n_valid):
    tri, ones_kk, sel, row_id = consts
    c_len = qc.shape[0]
    f = lb + (1.0 - lb) * _sigmoid(fc)
    if n_valid is not None:
        f = jnp.where(row_id < n_valid, f, 1.0)
    logf = jnp.log(f)
    kin = 1.0 - f
    g = _dot(tri, logf, precision=lax.Precision.HIGHEST)
    qs = qc * (HEAD_DIM ** -0.5)
    g_end = g[c_len - 1:c_len, :]
    qg = qs * jnp.exp(g)
    kd = kin * jnp.exp(g_end - g)
    st_b = st.astype(BF16)
    o_parts = []
    n_sub = c_len // HGRN_SUB
    sub_iota_s = lax.broadcasted_iota(I32, (HGRN_SUB, HEAD_DIM), 0)
    for i in range(n_sub):
        lo, hi = i * HGRN_SUB, (i + 1) * HGRN_SUB
        gi, qi, ki, vi = g[lo:hi], qs[lo:hi], kin[lo:hi], vc[lo:hi]
        rows = []
        for t in range(HGRN_SUB):
            dt = jnp.where(sub_iota_s <= t, gi[t:t + 1, :] - gi, NEG_BIG)
            rows.append(jnp.exp(dt) * qi[t:t + 1, :] * ki)
        x2 = jnp.concatenate(rows, axis=0)
        y = _dot(x2.astype(BF16), ones_kk)
        z = y * jnp.concatenate([vi] * HGRN_SUB, axis=0)
        o_i = _dot(sel, z.astype(BF16))
        o_i = o_i + _dot_nt(qg[lo:hi].astype(BF16), st_b)
        if i > 0:
            b_row = g[lo - 1:lo, :]
            qt = qi * jnp.exp(gi - b_row)
            kt = kin[:lo] * jnp.exp(b_row - g[:lo])
            a = _dot_nt(qt.astype(BF16), kt.astype(BF16))
            o_i = o_i + _dot(a.astype(BF16), vc[:lo].astype(BF16))
        o_parts.append(o_i)
    o = jnp.concatenate(o_parts, axis=0) if n_sub > 1 else o_parts[0]
    st_new = st * jnp.exp(g_end) + _dot_tn(vc.astype(BF16), kd.astype(BF16))
    return o, st_new


def _hgrn_consts(c_len):
    r = lax.broadcasted_iota(I32, (c_len, c_len), 0)
    c = lax.broadcasted_iota(I32, (c_len, c_len), 1)
    tri = (c <= r).astype(F32)
    ones_kk = jnp.ones((HEAD_DIM, HEAD_DIM), BF16)
    sr = lax.broadcasted_iota(I32, (HGRN_SUB, HGRN_SUB * HGRN_SUB), 0)
    sc = lax.broadcasted_iota(I32, (HGRN_SUB, HGRN_SUB * HGRN_SUB), 1)
    sel = ((sc // HGRN_SUB) == sr).astype(BF16)
    row_id = lax.broadcasted_iota(I32, (c_len, HEAD_DIM), 0)
    return tri, ones_kk, sel, row_id


def _hgrn_epilogue(o, ga, go):
    y = o * lax.rsqrt(jnp.mean(o * o, axis=-1, keepdims=True) + NORM_EPS) * go
    return (y * (ga * _sigmoid(ga))).astype(BF16)


def _hgrn_kernel(*refs, heads, c_len, n_chunks, n_valid, has_state):
    if has_state:
        q_ref, f_ref, v_ref, g_ref, lb_ref, go_ref, s0_ref, o_ref, s_ref = refs
    else:
        q_ref, f_ref, v_ref, g_ref, lb_ref, go_ref, o_ref, s_ref = refs
    consts = _hgrn_consts(c_len)
    go = go_ref[...]
    pad = c_len - q_ref.shape[1] if n_chunks == 1 else 0

    def load(ref, rows, hl):
        x = ref[0, rows, hl]
        if pad:
            x = jnp.concatenate([x, jnp.zeros((pad, HEAD_DIM), F32)], axis=0)
        return x

    for h in range(heads):
        hl = slice(h * HEAD_DIM, (h + 1) * HEAD_DIM)
        lb = lb_ref[:, hl]
        st0 = jnp.transpose(s0_ref[0, h]) if has_state else jnp.zeros((HEAD_DIM, HEAD_DIM), F32)
        if n_chunks == 1:
            rows = slice(0, q_ref.shape[1])
            o, st = _hgrn_chunk(load(q_ref, rows, hl), load(f_ref, rows, hl), load(v_ref, rows, hl),
                                lb, st0, consts, n_valid)
            nr = q_ref.shape[1]
            o_ref[0, :, hl] = _hgrn_epilogue(o[:nr], g_ref[0, :, hl], go)
        else:
            def body(ci, st, hl=hl, lb=lb):
                rows = pl.ds(pl.multiple_of(ci * c_len, c_len), c_len)
                o, st = _hgrn_chunk(q_ref[0, rows, hl], f_ref[0, rows, hl], v_ref[0, rows, hl],
                                    lb, st, consts, None)
                o_ref[0, rows, hl] = _hgrn_epilogue(o, g_ref[0, rows, hl], go)
                return st
            st = lax.fori_loop(0, n_chunks, body, st0)
        s_ref[0, h] = jnp.transpose(st)


def _hgrn(u3, col0, lb, go, s0, heads_total, heads_blk, c_len, n_valid):
    nb, l, _ = u3.shape
    w = heads_blk * HEAD_DIM
    hw = heads_total * HEAD_DIM
    nhb = heads_total // heads_blk
    has_state = s0 is not None
    n_chunks = max(1, l // c_len)

    def uspec(k):
        off = (col0 + k * hw) // w
        return pl.BlockSpec((1, l, w), functools.partial(lambda b, h, off: (b, 0, off + h), off=off))

    in_specs = [uspec(0), uspec(1), uspec(2), uspec(3),
                pl.BlockSpec((1, w), lambda b, h: (0, h)),
                pl.BlockSpec((1, HEAD_DIM), lambda b, h: (0, 0))]
    args = [u3, u3, u3, u3, lb.reshape(1, hw), go.reshape(1, HEAD_DIM)]
    sspec = pl.BlockSpec((1, heads_blk, HEAD_DIM, HEAD_DIM), lambda b, h: (b, h, 0, 0))
    if has_state:
        in_specs.append(sspec)
        args.append(s0)
    return pl.pallas_call(
        functools.partial(_hgrn_kernel, heads=heads_blk, c_len=c_len, n_chunks=n_chunks,
                          n_valid=n_valid, has_state=has_state),
        grid=(nb, nhb),
        in_specs=in_specs,
        out_specs=[pl.BlockSpec((1, l, w), lambda b, h: (b, 0, h)), sspec],
        out_shape=[jax.ShapeDtypeStruct((nb, l, hw), BF16),
                   jax.ShapeDtypeStruct((nb, heads_total, HEAD_DIM, HEAD_DIM), F32)],
        compiler_params=_cparams(("arbitrary", "arbitrary")),
        name="hgrn",
    )(*args)


def _rope_tables(pos, rot):
    half = rot // 2
    inv = ROPE_THETA ** (-jnp.arange(half, dtype=F32) * 2.0 / rot)
    ang = pos.astype(F32)[:, None] * inv[None, :]
    cos, sin = jnp.cos(ang), jnp.sin(ang)
    n = pos.shape[0]
    rest = HEAD_DIM - rot
    cosf = jnp.concatenate([cos, cos, jnp.ones((n, rest), F32)], axis=1)
    sinf = jnp.concatenate([-sin, sin, jnp.zeros((n, rest), F32)], axis=1)
    return cosf, sinf


def _rope(x, cosf, sinf, half):
    ax = x.ndim - 1
    lane = lax.broadcasted_iota(I32, x.shape, ax)
    partner = jnp.where(lane < half, pltpu.roll(x, HEAD_DIM - half, ax), pltpu.roll(x, half, ax))
    return x * cosf + partner * sinf


def _head_rms(x, g):
    return x * lax.rsqrt(jnp.mean(x * x, axis=-1, keepdims=True) + NORM_EPS) * g


def _prep_kernel(*refs, hb, kv, n_qi):
    q_ref, k_ref, v_ref = refs[:3]
    qi_refs = refs[3:3 + n_qi]
    (t_ref, cq_ref, sq_ref, ci_ref, si_ref, gq_ref, gk_ref, gi_ref,
     qn_ref, kn_ref, k16_ref, vo_ref, v16_ref, qir_ref, kio_ref, ki16_ref, w_ref) = refs[3 + n_qi:]
    cq, sq, ci, si = cq_ref[...], sq_ref[...], ci_ref[...], si_ref[...]
    gq, gk, gi = gq_ref[...], gk_ref[...], gi_ref[...]
    for h in range(hb):
        hl = slice(h * HEAD_DIM, (h + 1) * HEAD_DIM)
        x = _rope(_head_rms(q_ref[:, :, hl], gq), cq, sq, ROT_DIM // 2)
        qn_ref[:, :, hl] = (x * (HEAD_DIM ** -0.5)).astype(BF16)
    for h in range(kv):
        hl = slice(h * HEAD_DIM, (h + 1) * HEAD_DIM)
        x = _rope(_head_rms(k_ref[:, :, hl], gk), cq, sq, ROT_DIM // 2)
        kn_ref[:, :, hl] = x
        k16_ref[:, :, hl] = x.astype(BF16)
    v = v_ref[...]
    vo_ref[...] = v
    v16_ref[...] = v.astype(BF16)
    per = IDX_HEADS // n_qi
    for h in range(IDX_HEADS):
        hl = slice(h * IDX_DIM, (h + 1) * IDX_DIM)
        sl = slice((h % per) * IDX_DIM, (h % per + 1) * IDX_DIM)
        qir_ref[:, :, hl] = _rope(qi_refs[h // per][:, :, sl], ci, si, IDX_ROT_DIM // 2).astype(BF16)
    t = t_ref[...]
    x = _rope(_head_rms(t[:, :, :IDX_DIM], gi), ci, si, IDX_ROT_DIM // 2)
    kio_ref[...] = x
    ki16_ref[...] = x.astype(BF16)
    w_ref[...] = t[:, :, IDX_DIM:] * (IDX_HEADS ** -0.5 * IDX_DIM ** -0.5)


def _prep(u3, ut3, col_qb, hb, kv, pos, g_q, g_k, g_kidx, bb, tl):
    nb, l, _ = u3.shape
    cq, sq = _rope_tables(pos, ROT_DIM)
    ci, si = _rope_tables(pos, IDX_ROT_DIM)
    wq, wk, wi = hb * HEAD_DIM, kv * HEAD_DIM, IDX_HEADS * IDX_DIM
    col_kb, col_vb, col_qi = col_qb + wq, col_qb + wq + wk, col_qb + wq + 2 * wk

    def cs(width, col):
        assert col % width == 0
        return pl.BlockSpec((bb, tl, width), functools.partial(lambda b, i, o: (b, i, o), o=col // width))

    def osp(width):
        return pl.BlockSpec((bb, tl, width), lambda b, i: (b, i, 0))

    tab = pl.BlockSpec((1, tl, HEAD_DIM), lambda b, i: (0, i, 0))
    gsp = pl.BlockSpec((1, 1, HEAD_DIM), lambda b, i: (0, 0, 0))
    tw = ut3.shape[-1]
    qiw = math.gcd(col_qi, wi)
    n_qi = wi // qiw
    out_shapes = [((nb, l, wq), BF16), ((nb, l, wk), F32), ((nb, l, wk), BF16), ((nb, l, wk), F32),
                  ((nb, l, wk), BF16), ((nb, l, wi), BF16), ((nb, l, IDX_DIM), F32), ((nb, l, IDX_DIM), BF16),
                  ((nb, l, IDX_HEADS), F32)]
    return pl.pallas_call(
        functools.partial(_prep_kernel, hb=hb, kv=kv, n_qi=n_qi),
        grid=(nb // bb, l // tl),
        in_specs=[cs(wq, col_qb), cs(wk, col_kb), cs(wk, col_vb)]
                 + [cs(qiw, col_qi + i * qiw) for i in range(n_qi)]
                 + [pl.BlockSpec((bb, tl, tw), lambda b, i: (b, i, 0)),
                    tab, tab, tab, tab, gsp, gsp, gsp],
        out_specs=[osp(s[0][2]) for s in out_shapes],
        out_shape=[jax.ShapeDtypeStruct(*s) for s in out_shapes],
        compiler_params=_cparams(("arbitrary", "arbitrary")),
        name="dsa_prep",
    )(u3, u3, u3, *([u3] * n_qi), ut3, cq[None], sq[None], ci[None], si[None],
      g_q.reshape(1, 1, HEAD_DIM), g_k.reshape(1, 1, HEAD_DIM), g_kidx.reshape(1, 1, IDX_DIM))


def _sortable_key(score):
    b = lax.bitcast_convert_type(score, I32)
    return jnp.where(b < 0, b ^ jnp.int32(0x7FFFFFFF), b)


def _select_topk(key_ref, n_sel, tie_u):
    r, s = key_ref.shape
    int_min = jnp.int32(-2 ** 31)

    def count_ge(t):
        return jnp.sum((key_ref[...] >= t).astype(I32), axis=1, keepdims=True)

    t0 = jnp.where(count_ge(jnp.zeros((r, 1), I32)) >= n_sel, jnp.int32(0), int_min)

    def body(i, t):
        cand = t | (jnp.int32(1) << (jnp.int32(30) - i))
        return jnp.where(count_ge(cand) >= n_sel, cand, t)

    t = lax.fori_loop(0, 31, body, t0)
    keys = key_ref[...]
    gt = keys > t
    need = (n_sel - jnp.sum(gt.astype(I32), axis=1, keepdims=True)).astype(F32)
    parts = []
    running = jnp.zeros((r, 1), F32)
    for j in range(s // LANES):
        sl = slice(j * LANES, (j + 1) * LANES)
        tie = (keys[:, sl] == t).astype(F32)
        pref = _dot(tie.astype(BF16), tie_u) + running
        running = running + jnp.sum(tie, axis=1, keepdims=True)
        parts.append(jnp.where(gt[:, sl] | ((tie > 0) & (pref <= need)), 1.0, 0.0))
    return jnp.concatenate(parts, axis=1)


def _tie_matrix():
    r = lax.broadcasted_iota(I32, (LANES, LANES), 0)
    c = lax.broadcasted_iota(I32, (LANES, LANES), 1)
    return (r <= c).astype(BF16)


def _dsa_prompt_kernel(qi_ref, w_ref, q_ref, ki_ref, k_ref, v_ref, o_ref, acc_ref, key_ref, bias_ref,
                       *, hb, kv, n_sel, tq):
    s_len = ki_ref.shape[1]
    ki = ki_ref[0]
    w = w_ref[0]
    acc_ref[...] = jnp.zeros_like(acc_ref)
    for h in range(IDX_HEADS):
        d = _dot_nt(qi_ref[0, :, h * IDX_DIM:(h + 1) * IDX_DIM], ki)
        acc_ref[...] += w[:, h:h + 1] * jnp.maximum(d, 0.0)
    q0 = pl.program_id(1) * tq
    qpos = q0 + lax.broadcasted_iota(I32, (tq, s_len), 0)
    kpos = lax.broadcasted_iota(I32, (tq, s_len), 1)
    causal = kpos <= qpos
    key_ref[...] = _sortable_key(jnp.where(causal, acc_ref[...], -jnp.inf))
    sel = _select_topk(key_ref, n_sel, _tie_matrix())
    bias_ref[...] = jnp.where((sel > 0) & causal, 0.0, NEG_BIG)
    rep = hb // kv
    for h in range(hb):
        g = h // rep
        kg = k_ref[0, :, g * HEAD_DIM:(g + 1) * HEAD_DIM]
        vg = v_ref[0, :, g * HEAD_DIM:(g + 1) * HEAD_DIM]
        lg = _dot_nt(q_ref[0, :, h * HEAD_DIM:(h + 1) * HEAD_DIM], kg) + bias_ref[...]
        m = jnp.max(lg, axis=1, keepdims=True)
        p = jnp.exp(lg - m)
        den = jnp.sum(p, axis=1, keepdims=True)
        o = _dot(p.astype(BF16), vg) / den
        o_ref[0, :, h * HEAD_DIM:(h + 1) * HEAD_DIM] = o.astype(BF16)


def _dsa_prompt(qir, wis, qn, ki16, k16, v16, hb, kv, tq):
    nb, l, _ = qn.shape
    n_sel = min(TOPK_MAX, l // 4)

    def qs(width):
        return pl.BlockSpec((1, tq, width), lambda b, i: (b, i, 0))

    def fs(width):
        return pl.BlockSpec((1, l, width), lambda b, i: (b, 0, 0))

    return pl.pallas_call(
        functools.partial(_dsa_prompt_kernel, hb=hb, kv=kv, n_sel=n_sel, tq=tq),
        grid=(nb, l // tq),
        in_specs=[qs(IDX_HEADS * IDX_DIM), qs(IDX_HEADS), qs(hb * HEAD_DIM),
                  fs(IDX_DIM), fs(kv * HEAD_DIM), fs(kv * HEAD_DIM)],
        out_specs=qs(hb * HEAD_DIM),
        out_shape=jax.ShapeDtypeStruct((nb, l, hb * HEAD_DIM), BF16),
        scratch_shapes=[pltpu.VMEM((tq, l), F32), pltpu.VMEM((tq, l), I32), pltpu.VMEM((tq, l), F32)],
        compiler_params=_cparams(("arbitrary", "arbitrary")),
        name="dsa_prompt",
    )(qir, wis, qn, ki16, k16, v16)


def _dsa_sample_kernel(*refs, n_pages, hb, kv, n_sel, n_new):
    pt_ref = refs[0]
    del pt_ref
    kip = refs[1:1 + n_pages]
    kp = refs[1 + n_pages:1 + 2 * n_pages]
    vp = refs[1 + 2 * n_pages:1 + 3 * n_pages]
    kin_ref, kn_ref, vn_ref, qi_ref, w_ref, q_ref, o_ref, key_ref, bias_ref = refs[1 + 3 * n_pages:]
    page = kip[0].shape[1]
    past = n_pages * page
    t_pad = qi_ref.shape[1]
    zpad = jnp.zeros((LANES - t_pad, HEAD_DIM), F32)

    def padded_new(x):
        return jnp.concatenate([x.astype(F32), zpad], axis=0).astype(BF16)

    qi = qi_ref[0].astype(F32)
    qst = jnp.concatenate([qi[:, h * IDX_DIM:(h + 1) * IDX_DIM] for h in range(IDX_HEADS)], axis=0).astype(BF16)
    w = w_ref[0]
    wcol = jnp.concatenate([w[:, h:h + 1] for h in range(IDX_HEADS)], axis=0)
    ki_all = jnp.concatenate([r[0].astype(BF16) for r in kip] + [padded_new(kin_ref[0])], axis=0)
    d = wcol * jnp.maximum(_dot_nt(qst, ki_all), 0.0)
    score = d[0:t_pad]
    for h in range(1, IDX_HEADS):
        score = score + d[h * t_pad:(h + 1) * t_pad]
    s_tot = past + LANES
    trow = lax.broadcasted_iota(I32, (t_pad, s_tot), 0)
    col = lax.broadcasted_iota(I32, (t_pad, s_tot), 1)
    valid = (col < past) | ((col - past <= trow) & (col - past < n_new))
    key_ref[...] = _sortable_key(jnp.where(valid, score, -jnp.inf))
    sel = _select_topk(key_ref, n_sel, _tie_matrix())
    bias_ref[...] = jnp.where((sel > 0) & valid, 0.0, NEG_BIG)
    rep = hb // kv
    q = q_ref[0].astype(F32)
    bias = jnp.concatenate([bias_ref[...]] * rep, axis=0)
    for g in range(kv):
        gl = slice(g * HEAD_DIM, (g + 1) * HEAD_DIM)
        qg = jnp.concatenate([q[:, (g * rep + r) * HEAD_DIM:(g * rep + r + 1) * HEAD_DIM] for r in range(rep)],
                             axis=0).astype(BF16)
        kg = jnp.concatenate([r[0, :, gl].astype(BF16) for r in kp] + [padded_new(kn_ref[0, :, gl])], axis=0)
        vg = jnp.concatenate([r[0, :, gl].astype(BF16) for r in vp] + [padded_new(vn_ref[0, :, gl])], axis=0)
        lg = _dot_nt(qg, kg) + bias
        m = jnp.max(lg, axis=1, keepdims=True)
        p = jnp.exp(lg - m)
        den = jnp.sum(p, axis=1, keepdims=True)
        o = _dot(p.astype(BF16), vg) / den
        for r in range(rep):
            hh = g * rep + r
            o_ref[0, :, hh * HEAD_DIM:(hh + 1) * HEAD_DIM] = o[r * t_pad:(r + 1) * t_pad].astype(BF16)


def _dsa_sample(page_table, cache_kidx, cache_k, cache_v, ki16, k16, v16, qir, wis, qn, hb, kv, n_new):
    db, n_pages = page_table.shape
    n_pool, page = cache_kidx.shape[0], cache_kidx.shape[1]
    t_pad = qn.shape[1]
    ck = cache_k.reshape(n_pool, page, kv * HEAD_DIM)
    cv = cache_v.reshape(n_pool, page, kv * HEAD_DIM)
    n_sel = min(TOPK_MAX, (n_pages * page + n_new) // 4)

    def pspec(width, p):
        return pl.BlockSpec((1, page, width), functools.partial(lambda b, pt, p: (pt[b, p], 0, 0), p=p))

    def bspec(width):
        return pl.BlockSpec((1, t_pad, width), lambda b, pt: (b, 0, 0))

    in_specs = ([pspec(IDX_DIM, p) for p in range(n_pages)]
                + [pspec(kv * HEAD_DIM, p) for p in range(n_pages)]
                + [pspec(kv * HEAD_DIM, p) for p in range(n_pages)]
                + [bspec(IDX_DIM), bspec(kv * HEAD_DIM), bspec(kv * HEAD_DIM),
                   bspec(IDX_HEADS * IDX_DIM), bspec(IDX_HEADS), bspec(hb * HEAD_DIM)])
    s_tot = n_pages * page + LANES
    grid_spec = pltpu.PrefetchScalarGridSpec(
        num_scalar_prefetch=1,
        grid=(db,),
        in_specs=in_specs,
        out_specs=pl.BlockSpec((1, t_pad, hb * HEAD_DIM), lambda b, pt: (b, 0, 0)),
        scratch_shapes=[pltpu.VMEM((t_pad, s_tot), I32), pltpu.VMEM((t_pad, s_tot), F32)],
    )
    return pl.pallas_call(
        functools.partial(_dsa_sample_kernel, n_pages=n_pages, hb=hb, kv=kv, n_sel=n_sel, n_new=n_new),
        grid_spec=grid_spec,
        out_shape=jax.ShapeDtypeStruct((db, t_pad, hb * HEAD_DIM), BF16),
        compiler_params=_cparams(("arbitrary",)),
        name="dsa_sample",
    )(page_table, *([cache_kidx] * n_pages), *([ck] * n_pages), *([cv] * n_pages),
      ki16, k16, v16, qir, wis, qn)


def _resid_router_kernel(x_ref, o_ref, g1_ref, g_ref, sc_ref, sh_ref, wr_ref, br_ref,
                         x1_ref, h_ref, ids_ref, gates_ref):
    x1 = x_ref[...] + g1_ref[...] * o_ref[...]
    x1_ref[...] = x1
    h = _rms_mod(x1, g_ref[...], sc_ref[...], sh_ref[...])
    h_ref[...] = h.astype(BF16)
    bb, tl, d = h.shape
    n_e = wr_ref.shape[1]
    logits = _dot(h.reshape(bb * tl, d), wr_ref[...], precision=lax.Precision.HIGHEST) + br_ref[...]
    lane = lax.broadcasted_iota(I32, logits.shape, 1)
    vals, ids = [], []
    for _ in range(TOP_K):
        m = jnp.max(logits, axis=1, keepdims=True)
        idx = jnp.min(jnp.where(logits == m, lane, n_e), axis=1, keepdims=True)
        vals.append(m)
        ids.append(idx)
        logits = jnp.where(lane == idx, -jnp.inf, logits)
    es = [jnp.exp(v - vals[0]) for v in vals]
    tot = es[0] + es[1] + es[2] + es[3]
    ids_ref[...] = jnp.concatenate(ids, axis=1).reshape(bb, tl, TOP_K)
    gates_ref[...] = jnp.concatenate([e / tot for e in es], axis=1).reshape(bb, tl, TOP_K)


def _resid_router(x, o, g1, g, sc, sh, wr, br, bb, tl):
    nb, l, d = x.shape
    n_e = wr.shape[1]
    xs = pl.BlockSpec((bb, tl, d), lambda b, i: (b, i, 0))
    ms = pl.BlockSpec((bb, 1, d), lambda b, i: (b, 0, 0))
    ks = pl.BlockSpec((bb, tl, TOP_K), lambda b, i: (b, i, 0))
    return pl.pallas_call(
        _resid_router_kernel,
        grid=(nb // bb, l // tl),
        in_specs=[xs, xs, ms, pl.BlockSpec((1, 1, d), lambda b, i: (0, 0, 0)), ms, ms,
                  pl.BlockSpec((d, n_e), lambda b, i: (0, 0)), pl.BlockSpec((1, n_e), lambda b, i: (0, 0))],
        out_specs=[xs, xs, ks, ks],
        out_shape=[jax.ShapeDtypeStruct((nb, l, d), F32), jax.ShapeDtypeStruct((nb, l, d), BF16),
                   jax.ShapeDtypeStruct((nb, l, TOP_K), I32), jax.ShapeDtypeStruct((nb, l, TOP_K), F32)],
        compiler_params=_cparams(("arbitrary", "arbitrary")),
        name="resid_router",
    )(x, o, g1, g.reshape(1, 1, d), sc, sh, wr, br.reshape(1, n_e))


def _row_gather_kernel(n_ref, idx_ref, src_ref, dst_ref, sem, *, chunk):
    i = pl.program_id(0)
    base = i * chunk

    def copy(r):
        return pltpu.make_async_copy(src_ref.at[idx_ref[0, 0, r]], dst_ref.at[base + r], sem)

    @pl.when(base < n_ref[0])
    def _():
        def start(r, c):
            copy(r).start()
            return c
        lax.fori_loop(0, chunk, start, 0)

        def wait(r, c):
            copy(r).wait()
            return c
        lax.fori_loop(0, chunk, wait, 0)


def _row_gather(src3, idx, n_rows, n_out, chunk):
    n_chunks = n_out // chunk
    grid_spec = pltpu.PrefetchScalarGridSpec(
        num_scalar_prefetch=1,
        grid=(n_chunks,),
        in_specs=[pl.BlockSpec((1, 1, chunk), lambda i, n: (i, 0, 0), memory_space=pltpu.SMEM),
                  pl.BlockSpec(memory_space=pl.ANY)],
        out_specs=pl.BlockSpec(memory_space=pl.ANY),
        scratch_shapes=[pltpu.SemaphoreType.DMA],
    )
    return pl.pallas_call(
        functools.partial(_row_gather_kernel, chunk=chunk),
        grid_spec=grid_spec,
        out_shape=jax.ShapeDtypeStruct((n_out,) + src3.shape[1:], src3.dtype),
        compiler_params=_cparams(("arbitrary",)),
        name="moe_dispatch",
    )(n_rows.reshape(1), idx.reshape(n_chunks, 1, chunk), src3)


def _gmm1_kernel(e_ref, st_ref, ns_ref, x_ref, w_ref, b_ref, perm_ref, o_ref, wb_ref, *, ft):
    del e_ref, st_ref
    n = ns_ref[pl.program_id(0)]

    @pl.when(n > 0)
    def _():
        wb_ref[...] = w_ref[...].astype(BF16)
        bias = b_ref[...]
        perm = perm_ref[...]

        def body(r, c):
            rows = pl.ds(pl.multiple_of(r * MOE_SUB, MOE_SUB), MOE_SUB)
            hh = _dot(x_ref[rows, :], wb_ref[...]) + bias
            hp = _dot(hh.astype(BF16), perm)
            xg = jnp.minimum(hp[:, :ft], SWIGLU_LIMIT)
            xl = jnp.clip(hp[:, ft:], -SWIGLU_LIMIT, SWIGLU_LIMIT)
            a = xg * _sigmoid(SWIGLU_ALPHA * xg) * (xl + 1.0)
            o_ref[rows, :] = a.astype(BF16)
            return c
        lax.fori_loop(0, n, body, 0)


def _gmm2_kernel(e_ref, st_ref, ns_ref, a_ref, w_ref, b_ref, o_ref, wb_ref):
    del e_ref, st_ref
    n = ns_ref[pl.program_id(0)]

    @pl.when(n > 0)
    def _():
        wb_ref[...] = w_ref[...].astype(BF16)
        bias = b_ref[...]

        def body(r, c):
            rows = pl.ds(pl.multiple_of(r * MOE_SUB, MOE_SUB), MOE_SUB)
            o_ref[rows, :] = _dot(a_ref[rows, :], wb_ref[...]) + bias
            return c
        lax.fori_loop(0, n, body, 0)


def _moe_tile_maps(n_j):
    def j_eff(s, j, ns):
        return jnp.where(ns[s] > 0, j, n_j - 1)
    return j_eff


def _gmm1(x_sorted, w1, b1, sup_e, sup_st, sup_ns, rs, ft):
    p_alloc, d = x_sorted.shape
    n_e, _, f2 = w1.shape
    n_j = f2 // (2 * ft)
    s_max = sup_e.shape[0]
    j_eff = _moe_tile_maps(n_j)
    pr = np.arange(2 * ft)
    perm = np.zeros((2 * ft, 2 * ft), np.float32)
    perm[pr, np.where(pr % 2 == 0, pr // 2, ft + pr // 2)] = 1.0
    grid_spec = pltpu.PrefetchScalarGridSpec(
        num_scalar_prefetch=3,
        grid=(s_max, n_j),
        in_specs=[pl.BlockSpec((pl.Element(rs), pl.Element(d)), lambda s, j, e, st, ns: (st[s] * MOE_SUB, 0)),
                  pl.BlockSpec((None, d, 2 * ft), lambda s, j, e, st, ns: (e[s], 0, j_eff(s, j, ns))),
                  pl.BlockSpec((None, 1, 2 * ft), lambda s, j, e, st, ns: (e[s], 0, j_eff(s, j, ns))),
                  pl.BlockSpec((2 * ft, 2 * ft), lambda s, j, e, st, ns: (0, 0))],
        out_specs=pl.BlockSpec((pl.Element(rs), pl.Element(ft)),
                               lambda s, j, e, st, ns: (st[s] * MOE_SUB, j_eff(s, j, ns) * ft)),
        scratch_shapes=[pltpu.VMEM((d, 2 * ft), BF16)],
    )
    return pl.pallas_call(
        functools.partial(_gmm1_kernel, ft=ft),
        grid_spec=grid_spec,
        out_shape=jax.ShapeDtypeStruct((p_alloc, f2 // 2), BF16),
        compiler_params=_cparams(("arbitrary", "arbitrary")),
        name="moe_gmm1",
    )(sup_e, sup_st, sup_ns, x_sorted, w1, b1.reshape(n_e, 1, f2), jnp.asarray(perm, BF16))


def _gmm2(a_sorted, w2, b2, sup_e, sup_st, sup_ns, rs, tn):
    p_alloc, f = a_sorted.shape
    n_e, _, d = w2.shape
    n_j = d // tn
    s_max = sup_e.shape[0]
    j_eff = _moe_tile_maps(n_j)
    grid_spec = pltpu.PrefetchScalarGridSpec(
        num_scalar_prefetch=3,
        grid=(s_max, n_j),
        in_specs=[pl.BlockSpec((pl.Element(rs), pl.Element(f)), lambda s, j, e, st, ns: (st[s] * MOE_SUB, 0)),
                  pl.BlockSpec((None, f, tn), lambda s, j, e, st, ns: (e[s], 0, j_eff(s, j, ns))),
                  pl.BlockSpec((None, 1, tn), lambda s, j, e, st, ns: (e[s], 0, j_eff(s, j, ns)))],
        out_specs=pl.BlockSpec((pl.Element(rs), pl.Element(tn)),
                               lambda s, j, e, st, ns: (st[s] * MOE_SUB, j_eff(s, j, ns) * tn)),
        scratch_shapes=[pltpu.VMEM((f, tn), BF16)],
    )
    return pl.pallas_call(
        _gmm2_kernel,
        grid_spec=grid_spec,
        out_shape=jax.ShapeDtypeStruct((p_alloc, d), F32),
        compiler_params=_cparams(("arbitrary", "arbitrary")),
        name="moe_gmm2",
    )(sup_e, sup_st, sup_ns, a_sorted, w2, b2.reshape(n_e, 1, d))


def _combine_kernel(slot_ref, gate_ref, y_ref, o_ref, buf, sem, *, tt):
    def copy(r):
        return pltpu.make_async_copy(y_ref.at[slot_ref[0, 0, r]], buf.at[r], sem)

    def start(r, c):
        copy(r).start()
        return c
    lax.fori_loop(0, tt * TOP_K, start, 0)

    def wait(r, c):
        copy(r).wait()
        return c
    lax.fori_loop(0, tt * TOP_K, wait, 0)

    def body(t, c):
        acc = gate_ref[0, 0, t * TOP_K] * buf[t * TOP_K]
        for k in range(1, TOP_K):
            acc = acc + gate_ref[0, 0, t * TOP_K + k] * buf[t * TOP_K + k]
        o_ref[t] = acc
        return c
    lax.fori_loop(0, tt, body, 0)


def _combine(y3, slots, gates, tt):
    t_tot = slots.shape[0] // TOP_K
    s, ln = y3.shape[1:]
    n_t = t_tot // tt
    return pl.pallas_call(
        functools.partial(_combine_kernel, tt=tt),
        grid=(n_t,),
        in_specs=[pl.BlockSpec((1, 1, tt * TOP_K), lambda i: (i, 0, 0), memory_space=pltpu.SMEM),
                  pl.BlockSpec((1, 1, tt * TOP_K), lambda i: (i, 0, 0), memory_space=pltpu.SMEM),
                  pl.BlockSpec(memory_space=pl.ANY)],
        out_specs=pl.BlockSpec((tt, s, ln), lambda i: (i, 0, 0)),
        out_shape=jax.ShapeDtypeStruct((t_tot, s, ln), F32),
        scratch_shapes=[pltpu.VMEM((tt * TOP_K, s, ln), F32), pltpu.SemaphoreType.DMA],
        compiler_params=_cparams(("arbitrary",)),
        name="moe_combine",
    )(slots.reshape(n_t, 1, tt * TOP_K), gates.reshape(n_t, 1, tt * TOP_K), y3)


def _final_kernel(x_ref, m_ref, g_ref, o_ref):
    o_ref[...] = x_ref[...] + g_ref[...] * m_ref[...]


def _final(x1, m, g2, bb, tl):
    nb, l, d = x1.shape
    xs = pl.BlockSpec((bb, tl, d), lambda b, i: (b, i, 0))
    return pl.pallas_call(
        _final_kernel,
        grid=(nb // bb, l // tl),
        in_specs=[xs, xs, pl.BlockSpec((bb, 1, d), lambda b, i: (b, 0, 0))],
        out_specs=xs,
        out_shape=jax.ShapeDtypeStruct((nb, l, d), F32),
        compiler_params=_cparams(("arbitrary", "arbitrary")),
        name="final_resid",
    )(x1, m, g2)


def _routing(ids, n_e, rs):
    t_tot = ids.shape[0]
    n_asg = t_tot * TOP_K
    e = ids.reshape(n_asg)
    onehot = (e[:, None] == jnp.arange(n_e, dtype=I32)[None, :]).astype(I32)
    csum = jnp.cumsum(onehot, axis=0)
    counts = csum[-1]
    rank = jnp.sum(onehot * csum, axis=1) - 1
    pc = ((counts + MOE_SUB - 1) // MOE_SUB) * MOE_SUB
    gs = jnp.cumsum(pc) - pc
    slot = gs[e] + rank
    n_rows = jnp.sum(pc)
    p_max = ((n_asg + n_e * (MOE_SUB - 1)) // MOE_SUB + 1) * MOE_SUB
    p_alloc = p_max + rs
    tok = jnp.zeros((p_alloc,), I32).at[slot].set(jnp.arange(n_asg, dtype=I32) // TOP_K)
    subs = rs // MOE_SUB
    ns_e = pc // MOE_SUB
    nsup_e = (ns_e + subs - 1) // subs
    sup_end = jnp.cumsum(nsup_e)
    total = sup_end[-1]
    s_max = n_e + p_max // rs + 1
    s_idx = jnp.arange(s_max, dtype=I32)
    s_clip = jnp.minimum(s_idx, total - 1)
    e_s = jnp.minimum(jnp.searchsorted(sup_end, s_clip, side="right").astype(I32), n_e - 1)
    j_s = s_clip - (sup_end[e_s] - nsup_e[e_s])
    start = ((gs[e_s] + j_s * rs) // MOE_SUB).astype(I32)
    nsub = jnp.where(s_idx < total, jnp.clip(ns_e[e_s] - j_s * subs, 0, subs), 0).astype(I32)
    return slot.astype(I32), tok, n_rows.astype(I32), p_alloc, e_s, start, nsub


def kernel(x_prompt, x_sample, cache_k, cache_v, cache_kidx, state_hgrn, page_table, c_prompt, c_sample,
           w_ada, b_ada, norm1, norm2, w_in, w_out, hgrn_lb, g_hgrn_o, g_q, g_k, g_kidx,
           w_router, b_router, w1, b1, w2, b2):
    b_sz, seq, d = x_prompt.shape
    db, t_dec, _ = x_sample.shape
    depth = w_ada.shape[0]
    n_pages, page = page_table.shape[1], cache_k.shape[2]
    past_len = n_pages * page
    ha = d // (2 * HEAD_DIM)
    hb = d // (2 * HEAD_DIM)
    kv = hb // 4
    n_e = w_router.shape[2]
    hw = ha * HEAD_DIM
    col_qb = 4 * hw
    col_tail = col_qb + hb * HEAD_DIM + 2 * kv * HEAD_DIM + IDX_HEADS * IDX_DIM
    n_cols = w_in.shape[2]
    assert n_cols == col_tail + IDX_DIM + IDX_HEADS and t_dec <= DEC_PAD

    lb_cum = jnp.cumsum(jax.nn.softmax(hgrn_lb.astype(F32), axis=0), axis=0)
    pos_p = jnp.arange(seq)
    pos_s = past_len + jnp.arange(DEC_PAD)
    tl_p = _pick(seq, 256, 16)
    sb = _pick(db, 16, 1)
    c_prompt_len = _pick(seq, 64, HGRN_SUB)
    rs = MOE_SUB * MOE_SUBS_PER_SUPER

    xp = x_prompt
    xs = jnp.pad(x_sample, ((0, 0), (0, DEC_PAD - t_dec), (0, 0)))
    n_c = b_sz + db
    c_all = jnp.pad(jnp.concatenate([c_prompt, c_sample], axis=0), ((0, (-n_c) % SUBLANES), (0, 0)))
    outs = [[] for _ in range(8)]
    for l in range(depth):
        lb = lb_cum[l + 1] - lb_cum[0]
        mod = _ada(c_all, w_ada[l], b_ada[l])
        mp = [mod[:b_sz, i * d:(i + 1) * d][:, None, :] for i in range(6)]
        ms = [mod[b_sz:n_c, i * d:(i + 1) * d][:, None, :] for i in range(6)]
        w_tail = w_in[l][:, col_tail:]
        res = []
        for (x, m, nb, ln, bb, tl, pos, s0, heads_blk, c_len, n_valid, past) in (
                (xp, mp, b_sz, seq, 1, tl_p, pos_p, None, 2, c_prompt_len, None, False),
                (xs, ms, db, DEC_PAD, sb, DEC_PAD, pos_s, state_hgrn[l], ha, HGRN_SUB, t_dec, True)):
            h = _norm_mod(x, norm1[l], m[1], m[0], bb, tl)
            h2d = h.reshape(nb * ln, d)
            u = _proj([h2d], w_in[l], col_tail).reshape(nb, ln, col_tail)
            ut = _proj([h2d], w_tail, n_cols - col_tail).reshape(nb, ln, n_cols - col_tail)
            oa, s_new = _hgrn(u, 0, lb, g_hgrn_o[l], s0, ha, heads_blk, c_len, n_valid)
            qn, kn, k16, vo, v16, qir, kio, ki16, wis = _prep(u, ut, col_qb, hb, kv, pos, g_q[l], g_k[l],
                                                              g_kidx[l], bb, tl)
            if past:
                ob = _dsa_sample(page_table, cache_kidx[l], cache_k[l], cache_v[l], ki16, k16, v16,
                                 qir, wis, qn, hb, kv, t_dec)
            else:
                ob = _dsa_prompt(qir, wis, qn, ki16, k16, v16, hb, kv, _pick(ln, 256, 16))
            o = _proj([oa.reshape(nb * ln, hw), ob.reshape(nb * ln, hb * HEAD_DIM)], w_out[l], d)
            x1, h2, ids, gates = _resid_router(x, o.reshape(nb, ln, d), m[2], norm2[l], m[4], m[3],
                                               w_router[l], b_router[l], bb, tl)
            res.append((x1, h2, ids, gates, kn, vo, kio, s_new))
        (x1p, h2p, idp, gtp, knp, vop, kiop, snp), (x1s, h2s, ids_, gts, kns, vos, kios, sns) = res
        t_p, t_s = b_sz * seq, db * t_dec
        t_tot = t_p + t_s
        ids_all = jnp.concatenate([idp.reshape(t_p, TOP_K), ids_[:, :t_dec].reshape(t_s, TOP_K)], axis=0)
        gates_all = jnp.concatenate([gtp.reshape(t_p, TOP_K), gts[:, :t_dec].reshape(t_s, TOP_K)], axis=0)
        h_all = jnp.concatenate([h2p.reshape(t_p, d), h2s[:, :t_dec].reshape(t_s, d)], axis=0)
        slot, tok, n_rows, p_alloc, sup_e, sup_st, sup_ns = _routing(ids_all, n_e, rs)
        x_sorted = _row_gather(h_all.reshape(t_tot, d // LANES, LANES), tok, n_rows, p_alloc, MOE_SUB)
        a_sorted = _gmm1(x_sorted.reshape(p_alloc, d), w1[l], b1[l], sup_e, sup_st, sup_ns, rs,
                         _pick(w1.shape[3] // 2, 256, LANES))
        y_sorted = _gmm2(a_sorted, w2[l], b2[l], sup_e, sup_st, sup_ns, rs, _pick(d, 512, LANES))
        tt = 64
        t_padded = ((t_tot + tt - 1) // tt) * tt
        slot_p = jnp.pad(slot, (0, (t_padded - t_tot) * TOP_K))
        gate_p = jnp.pad(gates_all.reshape(-1), (0, (t_padded - t_tot) * TOP_K))
        m3 = _combine(y_sorted.reshape(p_alloc, d // LANES, LANES), slot_p, gate_p, tt)
        m_all = m3.reshape(t_padded, d)
        m_p = m_all[:t_p].reshape(b_sz, seq, d)
        m_s = jnp.pad(m_all[t_p:t_tot].reshape(db, t_dec, d), ((0, 0), (0, DEC_PAD - t_dec), (0, 0)))
        xp = _final(x1p, m_p, mp[5], 1, tl_p)
        xs = _final(x1s, m_s, ms[5], sb, DEC_PAD)
        for lst, v in zip(outs, (knp.reshape(b_sz, seq, kv, HEAD_DIM), vop.reshape(b_sz, seq, kv, HEAD_DIM), kiop,
                                 snp, kns[:, :t_dec].reshape(db, t_dec, kv, HEAD_DIM),
                                 vos[:, :t_dec].reshape(db, t_dec, kv, HEAD_DIM), kios[:, :t_dec], sns)):
            lst.append(v)
    return (xp, xs[:, :t_dec]) + tuple(jnp.stack(o) for o in outs)
```

```python
import functools
import math

import numpy as np
import jax
import jax.numpy as jnp
from jax import lax
from jax.experimental import pallas as pl
from jax.experimental.pallas import tpu as pltpu

F32 = jnp.float32
BF16 = jnp.bfloat16
I32 = jnp.int32

HEAD_DIM = 128
ROT_DIM = HEAD_DIM // 4
ROPE_THETA = 500000.0
IDX_HEADS = 32
IDX_DIM = 128
IDX_ROT_DIM = 64
TOPK_MAX = 256
TOP_K = 4
SWIGLU_LIMIT = 7.0
SWIGLU_ALPHA = 1.702
NORM_EPS = 1e-6
DEC_PAD = 8

LANES = 128
SUBLANES = 8
VMEM_LIMIT_BYTES = 56 * 1024 * 1024

NEG_BIG = -1e30
MOE_SUB = 256
MOE_SUBS_PER_SUPER = 6


def _cparams(sem):
    return pltpu.CompilerParams(dimension_semantics=sem, vmem_limit_bytes=VMEM_LIMIT_BYTES)


def _dot(a, b, precision=None):
    return jnp.dot(a, b, preferred_element_type=F32, precision=precision)


def _dot_nt(a, b):
    return lax.dot_general(a, b, (((1,), (1,)), ((), ())), preferred_element_type=F32)


def _dot_tn(a, b):
    return lax.dot_general(a, b, (((0,), (0,)), ((), ())), preferred_element_type=F32)


def _sigmoid(x):
    return 1.0 / (1.0 + jnp.exp(-x))


def _pick(n, pref, align):
    if n <= pref:
        return n
    t = (pref // align) * align
    while t >= align:
        if n % t == 0:
            return t
        t -= align
    return n


def _ada_kernel(c_ref, w_ref, b_ref, o_ref):
    c = c_ref[...]
    s = c * _sigmoid(c)
    o_ref[...] = _dot(s.astype(BF16), w_ref[...].astype(BF16)) + b_ref[...]


def _ada(c, w, b):
    m, d = c.shape
    n = w.shape[1]
    tn = _pick(n, 512, LANES)
    return pl.pallas_call(
        _ada_kernel,
        grid=(n // tn,),
        in_specs=[pl.BlockSpec((m, d), lambda j: (0, 0)),
                  pl.BlockSpec((d, tn), lambda j: (0, j)),
                  pl.BlockSpec((1, tn), lambda j: (0, j))],
        out_specs=pl.BlockSpec((m, tn), lambda j: (0, j)),
        out_shape=jax.ShapeDtypeStruct((m, n), F32),
        compiler_params=_cparams(("arbitrary",)),
        name="ada",
    )(c, w, b.reshape(1, n))


def _rms_mod(x, g, sc, sh):
    y = x * lax.rsqrt(jnp.mean(x * x, axis=-1, keepdims=True) + NORM_EPS)
    return (y * g) * (1.0 + sc) + sh


def _norm_mod_kernel(x_ref, g_ref, sc_ref, sh_ref, h_ref):
    h_ref[...] = _rms_mod(x_ref[...], g_ref[...], sc_ref[...], sh_ref[...]).astype(BF16)


def _norm_mod(x, g, sc, sh, bb, tl):
    nb, l, d = x.shape
    xs = pl.BlockSpec((bb, tl, d), lambda b, i: (b, i, 0))
    ms = pl.BlockSpec((bb, 1, d), lambda b, i: (b, 0, 0))
    return pl.pallas_call(
        _norm_mod_kernel,
        grid=(nb // bb, l // tl),
        in_specs=[xs, pl.BlockSpec((1, 1, d), lambda b, i: (0, 0, 0)), ms, ms],
        out_specs=xs,
        out_shape=jax.ShapeDtypeStruct((nb, l, d), BF16),
        compiler_params=_cparams(("arbitrary", "arbitrary")),
        name="norm_mod",
    )(x, g.reshape(1, 1, d), sc, sh)


def _proj_kernel(*refs, n_a):
    a_refs, w_refs, o_ref = refs[:n_a], refs[n_a:2 * n_a], refs[2 * n_a]
    acc = _dot(a_refs[0][...], w_refs[0][...].astype(BF16))
    for i in range(1, n_a):
        acc = acc + _dot(a_refs[i][...], w_refs[i][...].astype(BF16))
    o_ref[...] = acc


def _proj(a_list, w, n_cols, tm_pref=1024, tn_pref=512):
    n_a = len(a_list)
    m = a_list[0].shape[0]
    kk = a_list[0].shape[1]
    assert all(a.shape == (m, kk) for a in a_list) and w.shape[0] == n_a * kk
    tm = _pick(m, tm_pref, 16)
    tn = _pick(n_cols, tn_pref, LANES)
    in_specs = [pl.BlockSpec((tm, kk), lambda i, j: (i, 0)) for _ in range(n_a)]
    in_specs += [pl.BlockSpec((kk, tn), functools.partial(lambda i, j, r: (r, j), r=r)) for r in range(n_a)]
    return pl.pallas_call(
        functools.partial(_proj_kernel, n_a=n_a),
        grid=(m // tm, n_cols // tn),
        in_specs=in_specs,
        out_specs=pl.BlockSpec((tm, tn), lambda i, j: (i, j)),
        out_shape=jax.ShapeDtypeStruct((m, n_cols), F32),
        compiler_params=_cparams(("arbitrary", "arbitrary")),
        name="proj",
    )(*a_list, *([w] * n_a))


HGRN_SUB = 16


def _hgrn_chunk(qc, fc, vc, lb, st, consts, n_valid):
    tri, ones_kk, sel, row_id = consts
    c_len = qc.shape[0]
    f = lb + (1.0 - lb) * _sigmoid(fc)
    if n_valid is not None:
        f = jnp.where(row_id < n_valid, f, 1.0)
    logf = jnp.log(f)
    kin = 1.0 - f
    g = _dot(tri, logf, precision=lax.Precision.HIGHEST)
    qs = qc * (HEAD_DIM ** -0.5)
    g_end = g[c_len - 1:c_len, :]
    qg = qs * jnp.exp(g)
    kd = kin * jnp.exp(g_end - g)
    st_b = st.astype(BF16)
    o_parts = []
    n_sub = c_len // HGRN_SUB
    sub_iota_s = lax.broadcasted_iota(I32, (HGRN_SUB, HEAD_DIM), 0)
    for i in range(n_sub):
        lo, hi = i * HGRN_SUB, (i + 1) * HGRN_SUB
        gi, qi, ki, vi = g[lo:hi], qs[lo:hi], kin[lo:hi], vc[lo:hi]
        rows = []
        for t in range(HGRN_SUB):
            dt = jnp.where(sub_iota_s <= t, gi[t:t + 1, :] - gi, NEG_BIG)
            rows.append(jnp.exp(dt) * qi[t:t + 1, :] * ki)
        x2 = jnp.concatenate(rows, axis=0)
        y = _dot(x2.astype(BF16), ones_kk)
        z = y * jnp.concatenate([vi] * HGRN_SUB, axis=0)
        o_i = _dot(sel, z.astype(BF16))
        o_i = o_i + _dot_nt(qg[lo:hi].astype(BF16), st_b)
        if i > 0:
            b_row = g[lo - 1:lo, :]
            qt = qi * jnp.exp(gi - b_row)
            kt = kin[:lo] * jnp.exp(b_row - g[:lo])
            a = _dot_nt(qt.astype(BF16), kt.astype(BF16))
            o_i = o_i + _dot(a.astype(BF16), vc[:lo].astype(BF16))
        o_parts.append(o_i)
    o = jnp.concatenate(o_parts, axis=0) if n_sub > 1 else o_parts[0]
    st_new = st * jnp.exp(g_end) + _dot_tn(vc.astype(BF16), kd.astype(BF16))
    return o, st_new


def _hgrn_consts(c_len):
    r = lax.broadcasted_iota(I32, (c_len, c_len), 0)
    c = lax.broadcasted_iota(I32, (c_len, c_len), 1)
    tri = (c <= r).astype(F32)
    ones_kk = jnp.ones((HEAD_DIM, HEAD_DIM), BF16)
    sr = lax.broadcasted_iota(I32, (HGRN_SUB, HGRN_SUB * HGRN_SUB), 0)
    sc = lax.broadcasted_iota(I32, (HGRN_SUB, HGRN_SUB * HGRN_SUB), 1)
    sel = ((sc // HGRN_SUB) == sr).astype(BF16)
    row_id = lax.broadcasted_iota(I32, (c_len, HEAD_DIM), 0)
    return tri, ones_kk, sel, row_id


def _hgrn_epilogue(o, ga, go):
    y = o * lax.rsqrt(jnp.mean(o * o, axis=-1, keepdims=True) + NORM_EPS) * go
    return (y * (ga * _sigmoid(ga))).astype(BF16)


def _hgrn_kernel(*refs, heads, c_len, n_chunks, n_valid, has_state):
    if has_state:
        q_ref, f_ref, v_ref, g_ref, lb_ref, go_ref, s0_ref, o_ref, s_ref, st_ref = refs
    else:
        q_ref, f_ref, v_ref, g_ref, lb_ref, go_ref, o_ref, s_ref, st_ref = refs
    consts = _hgrn_consts(c_len)
    go = go_ref[...]
    n_rows = q_ref.shape[1]
    pad = c_len - n_rows if n_chunks == 1 else 0

    def load(ref, rows, hl):
        x = ref[0, rows, hl]
        if pad:
            x = jnp.concatenate([x, jnp.zeros((pad, HEAD_DIM), F32)], axis=0)
        return x

    for h in range(heads):
        st_ref[h] = jnp.transpose(s0_ref[0, h]) if has_state else jnp.zeros((HEAD_DIM, HEAD_DIM), F32)

    def all_heads(rows, nr):
        for h in range(heads):
            hl = slice(h * HEAD_DIM, (h + 1) * HEAD_DIM)
            o, st = _hgrn_chunk(load(q_ref, rows, hl), load(f_ref, rows, hl), load(v_ref, rows, hl),
                                lb_ref[:, hl], st_ref[h], consts, n_valid)
            st_ref[h] = st
            o_ref[0, rows, hl] = _hgrn_epilogue(o[:nr], g_ref[0, rows, hl], go)

    if n_chunks == 1:
        all_heads(slice(0, n_rows), n_rows)
    else:
        def body(ci, c):
            all_heads(pl.ds(pl.multiple_of(ci * c_len, c_len), c_len), c_len)
            return c
        lax.fori_loop(0, n_chunks, body, 0)
    for h in range(heads):
        s_ref[0, h] = jnp.transpose(st_ref[h])


def _hgrn(u3, col0, lb, go, s0, heads_total, heads_blk, c_len, n_valid):
    nb, l, _ = u3.shape
    w = heads_blk * HEAD_DIM
    hw = heads_total * HEAD_DIM
    nhb = heads_total // heads_blk
    has_state = s0 is not None
    n_chunks = max(1, l // c_len)

    def uspec(k):
        off = (col0 + k * hw) // w
        return pl.BlockSpec((1, l, w), functools.partial(lambda b, h, off: (b, 0, off + h), off=off))

    in_specs = [uspec(0), uspec(1), uspec(2), uspec(3),
                pl.BlockSpec((1, w), lambda b, h: (0, h)),
                pl.BlockSpec((1, HEAD_DIM), lambda b, h: (0, 0))]
    args = [u3, u3, u3, u3, lb.reshape(1, hw), go.reshape(1, HEAD_DIM)]
    sspec = pl.BlockSpec((1, heads_blk, HEAD_DIM, HEAD_DIM), lambda b, h: (b, h, 0, 0))
    if has_state:
        in_specs.append(sspec)
        args.append(s0)
    return pl.pallas_call(
        functools.partial(_hgrn_kernel, heads=heads_blk, c_len=c_len, n_chunks=n_chunks,
                          n_valid=n_valid, has_state=has_state),
        grid=(nb, nhb),
        in_specs=in_specs,
        out_specs=[pl.BlockSpec((1, l, w), lambda b, h: (b, 0, h)), sspec],
        out_shape=[jax.ShapeDtypeStruct((nb, l, hw), BF16),
                   jax.ShapeDtypeStruct((nb, heads_total, HEAD_DIM, HEAD_DIM), F32)],
        scratch_shapes=[pltpu.VMEM((heads_blk, HEAD_DIM, HEAD_DIM), F32)],
        compiler_params=_cparams(("arbitrary", "arbitrary")),
        name="hgrn",
    )(*args)


def _rope_tables(pos, rot):
    half = rot // 2
    inv = ROPE_THETA ** (-jnp.arange(half, dtype=F32) * 2.0 / rot)
    ang = pos.astype(F32)[:, None] * inv[None, :]
    cos, sin = jnp.cos(ang), jnp.sin(ang)
    n = pos.shape[0]
    rest = HEAD_DIM - rot
    cosf = jnp.concatenate([cos, cos, jnp.ones((n, rest), F32)], axis=1)
    sinf = jnp.concatenate([-sin, sin, jnp.zeros((n, rest), F32)], axis=1)
    return cosf, sinf


def _rope(x, cosf, sinf, half):
    ax = x.ndim - 1
    lane = lax.broadcasted_iota(I32, x.shape, ax)
    partner = jnp.where(lane < half, pltpu.roll(x, HEAD_DIM - half, ax), pltpu.roll(x, half, ax))
    return x * cosf + partner * sinf


def _head_rms(x, g):
    return x * lax.rsqrt(jnp.mean(x * x, axis=-1, keepdims=True) + NORM_EPS) * g


def _prep_kernel(*refs, hb, kv, n_qi):
    q_ref, k_ref, v_ref = refs[:3]
    qi_refs = refs[3:3 + n_qi]
    (t_ref, cq_ref, sq_ref, ci_ref, si_ref, gq_ref, gk_ref, gi_ref,
     qn_ref, kn_ref, k16_ref, vo_ref, v16_ref, qir_ref, kio_ref, ki16_ref, w_ref) = refs[3 + n_qi:]
    cq, sq, ci, si = cq_ref[...], sq_ref[...], ci_ref[...], si_ref[...]
    gq, gk, gi = gq_ref[...], gk_ref[...], gi_ref[...]
    for h in range(hb):
        hl = slice(h * HEAD_DIM, (h + 1) * HEAD_DIM)
        x = _rope(_head_rms(q_ref[:, :, hl], gq), cq, sq, ROT_DIM // 2)
        qn_ref[:, :, hl] = (x * (HEAD_DIM ** -0.5)).astype(BF16)
    for h in range(kv):
        hl = slice(h * HEAD_DIM, (h + 1) * HEAD_DIM)
        x = _rope(_head_rms(k_ref[:, :, hl], gk), cq, sq, ROT_DIM // 2)
        kn_ref[:, :, hl] = x
        k16_ref[:, :, hl] = x.astype(BF16)
    v = v_ref[...]
    vo_ref[...] = v
    v16_ref[...] = v.astype(BF16)
    per = IDX_HEADS // n_qi
    for h in range(IDX_HEADS):
        hl = slice(h * IDX_DIM, (h + 1) * IDX_DIM)
        sl = slice((h % per) * IDX_DIM, (h % per + 1) * IDX_DIM)
        qir_ref[:, :, hl] = _rope(qi_refs[h // per][:, :, sl], ci, si, IDX_ROT_DIM // 2).astype(BF16)
    t = t_ref[...]
    x = _rope(_head_rms(t[:, :, :IDX_DIM], gi), ci, si, IDX_ROT_DIM // 2)
    kio_ref[...] = x
    ki16_ref[...] = x.astype(BF16)
    w_ref[...] = t[:, :, IDX_DIM:] * (IDX_HEADS ** -0.5 * IDX_DIM ** -0.5)


def _prep(u3, ut3, col_qb, hb, kv, pos, g_q, g_k, g_kidx, bb, tl):
    nb, l, _ = u3.shape
    cq, sq = _rope_tables(pos, ROT_DIM)
    ci, si = _rope_tables(pos, IDX_ROT_DIM)
    wq, wk, wi = hb * HEAD_DIM, kv * HEAD_DIM, IDX_HEADS * IDX_DIM
    col_kb, col_vb, col_qi = col_qb + wq, col_qb + wq + wk, col_qb + wq + 2 * wk

    def cs(width, col):
        assert col % width == 0
        return pl.BlockSpec((bb, tl, width), functools.partial(lambda b, i, o: (b, i, o), o=col // width))

    def osp(width):
        return pl.BlockSpec((bb, tl, width), lambda b, i: (b, i, 0))

    tab = pl.BlockSpec((1, tl, HEAD_DIM), lambda b, i: (0, i, 0))
    gsp = pl.BlockSpec((1, 1, HEAD_DIM), lambda b, i: (0, 0, 0))
    tw = ut3.shape[-1]
    qiw = math.gcd(col_qi, wi)
    n_qi = wi // qiw
    out_shapes = [((nb, l, wq), BF16), ((nb, l, wk), F32), ((nb, l, wk), BF16), ((nb, l, wk), F32),
                  ((nb, l, wk), BF16), ((nb, l, wi), BF16), ((nb, l, IDX_DIM), F32), ((nb, l, IDX_DIM), BF16),
                  ((nb, l, IDX_HEADS), F32)]
    return pl.pallas_call(
        functools.partial(_prep_kernel, hb=hb, kv=kv, n_qi=n_qi),
        grid=(nb // bb, l // tl),
        in_specs=[cs(wq, col_qb), cs(wk, col_kb), cs(wk, col_vb)]
                 + [cs(qiw, col_qi + i * qiw) for i in range(n_qi)]
                 + [pl.BlockSpec((bb, tl, tw), lambda b, i: (b, i, 0)),
                    tab, tab, tab, tab, gsp, gsp, gsp],
        out_specs=[osp(s[0][2]) for s in out_shapes],
        out_shape=[jax.ShapeDtypeStruct(*s) for s in out_shapes],
        compiler_params=_cparams(("arbitrary", "arbitrary")),
        name="dsa_prep",
    )(u3, u3, u3, *([u3] * n_qi), ut3, cq[None], sq[None], ci[None], si[None],
      g_q.reshape(1, 1, HEAD_DIM), g_k.reshape(1, 1, HEAD_DIM), g_kidx.reshape(1, 1, IDX_DIM))


def _sortable_key(score):
    b = lax.bitcast_convert_type(score, I32)
    return jnp.where(b < 0, b ^ jnp.int32(0x7FFFFFFF), b)


def _select_topk(key_ref, n_sel, tie_u):
    r, s = key_ref.shape
    int_min = jnp.int32(-2 ** 31)

    def count_ge(t):
        return jnp.sum((key_ref[...] >= t).astype(I32), axis=1, keepdims=True)

    t0 = jnp.where(count_ge(jnp.zeros((r, 1), I32)) >= n_sel, jnp.int32(0), int_min)

    def body(i, t):
        cand = t | (jnp.int32(1) << (jnp.int32(30) - i))
        return jnp.where(count_ge(cand) >= n_sel, cand, t)

    t = lax.fori_loop(0, 31, body, t0)
    keys = key_ref[...]
    gt = keys > t
    need = (n_sel - jnp.sum(gt.astype(I32), axis=1, keepdims=True)).astype(F32)
    parts = []
    running = jnp.zeros((r, 1), F32)
    for j in range(s // LANES):
        sl = slice(j * LANES, (j + 1) * LANES)
        tie = (keys[:, sl] == t).astype(F32)
        pref = _dot(tie.astype(BF16), tie_u) + running
        running = running + jnp.sum(tie, axis=1, keepdims=True)
        parts.append(jnp.where(gt[:, sl] | ((tie > 0) & (pref <= need)), 1.0, 0.0))
    return jnp.concatenate(parts, axis=1)


def _tie_matrix():
    r = lax.broadcasted_iota(I32, (LANES, LANES), 0)
    c = lax.broadcasted_iota(I32, (LANES, LANES), 1)
    return (r <= c).astype(BF16)


def _dsa_prompt_kernel(qi_ref, w_ref, q_ref, ki_ref, k_ref, v_ref, o_ref, acc_ref, key_ref, bias_ref,
                       *, hb, kv, n_sel, tq):
    s_len = ki_ref.shape[1]
    ki = ki_ref[0]
    w = w_ref[0]
    acc_ref[...] = jnp.zeros_like(acc_ref)
    for h in range(IDX_HEADS):
        d = _dot_nt(qi_ref[0, :, h * IDX_DIM:(h + 1) * IDX_DIM], ki)
        acc_ref[...] += w[:, h:h + 1] * jnp.maximum(d, 0.0)
    q0 = pl.program_id(1) * tq
    qpos = q0 + lax.broadcasted_iota(I32, (tq, s_len), 0)
    kpos = lax.broadcasted_iota(I32, (tq, s_len), 1)
    causal = kpos <= qpos
    key_ref[...] = _sortable_key(jnp.where(causal, acc_ref[...], -jnp.inf))
    sel = _select_topk(key_ref, n_sel, _tie_matrix())
    bias_ref[...] = jnp.where((sel > 0) & causal, 0.0, NEG_BIG)
    rep = hb // kv
    for h in range(hb):
        g = h // rep
        kg = k_ref[0, :, g * HEAD_DIM:(g + 1) * HEAD_DIM]
        vg = v_ref[0, :, g * HEAD_DIM:(g + 1) * HEAD_DIM]
        lg = _dot_nt(q_ref[0, :, h * HEAD_DIM:(h + 1) * HEAD_DIM], kg) + bias_ref[...]
        m = jnp.max(lg, axis=1, keepdims=True)
        p = jnp.exp(lg - m)
        den = jnp.sum(p, axis=1, keepdims=True)
        o = _dot(p.astype(BF16), vg) / den
        o_ref[0, :, h * HEAD_DIM:(h + 1) * HEAD_DIM] = o.astype(BF16)


def _dsa_prompt(qir, wis, qn, ki16, k16, v16, hb, kv, tq):
    nb, l, _ = qn.shape
    n_sel = min(TOPK_MAX, l // 4)

    def qs(width):
        return pl.BlockSpec((1, tq, width), lambda b, i: (b, i, 0))

    def fs(width):
        return pl.BlockSpec((1, l, width), lambda b, i: (b, 0, 0))

    return pl.pallas_call(
        functools.partial(_dsa_prompt_kernel, hb=hb, kv=kv, n_sel=n_sel, tq=tq),
        grid=(nb, l // tq),
        in_specs=[qs(IDX_HEADS * IDX_DIM), qs(IDX_HEADS), qs(hb * HEAD_DIM),
                  fs(IDX_DIM), fs(kv * HEAD_DIM), fs(kv * HEAD_DIM)],
        out_specs=qs(hb * HEAD_DIM),
        out_shape=jax.ShapeDtypeStruct((nb, l, hb * HEAD_DIM), BF16),
        scratch_shapes=[pltpu.VMEM((tq, l), F32), pltpu.VMEM((tq, l), I32), pltpu.VMEM((tq, l), F32)],
        compiler_params=_cparams(("arbitrary", "arbitrary")),
        name="dsa_prompt",
    )(qir, wis, qn, ki16, k16, v16)


def _dsa_sample_kernel(*refs, n_pages, hb, kv, n_sel, n_new):
    pt_ref = refs[0]
    del pt_ref
    kip = refs[1:1 + n_pages]
    kp = refs[1 + n_pages:1 + 2 * n_pages]
    vp = refs[1 + 2 * n_pages:1 + 3 * n_pages]
    kin_ref, kn_ref, vn_ref, qi_ref, w_ref, q_ref, o_ref, key_ref, bias_ref = refs[1 + 3 * n_pages:]
    page = kip[0].shape[1]
    past = n_pages * page
    t_pad = qi_ref.shape[1]
    zpad = jnp.zeros((LANES - t_pad, HEAD_DIM), F32)

    def padded_new(x):
        return jnp.concatenate([x.astype(F32), zpad], axis=0).astype(BF16)

    qi = qi_ref[0].astype(F32)
    qst = jnp.concatenate([qi[:, h * IDX_DIM:(h + 1) * IDX_DIM] for h in range(IDX_HEADS)], axis=0).astype(BF16)
    w = w_ref[0]
    wcol = jnp.concatenate([w[:, h:h + 1] for h in range(IDX_HEADS)], axis=0)
    ki_all = jnp.concatenate([r[0].astype(BF16) for r in kip] + [padded_new(kin_ref[0])], axis=0)
    d = wcol * jnp.maximum(_dot_nt(qst, ki_all), 0.0)
    score = d[0:t_pad]
    for h in range(1, IDX_HEADS):
        score = score + d[h * t_pad:(h + 1) * t_pad]
    s_tot = past + LANES
    trow = lax.broadcasted_iota(I32, (t_pad, s_tot), 0)
    col = lax.broadcasted_iota(I32, (t_pad, s_tot), 1)
    valid = (col < past) | ((col - past <= trow) & (col - past < n_new))
    key_ref[...] = _sortable_key(jnp.where(valid, score, -jnp.inf))
    sel = _select_topk(key_ref, n_sel, _tie_matrix())
    bias_ref[...] = jnp.where((sel > 0) & valid, 0.0, NEG_BIG)
    rep = hb // kv
    q = q_ref[0].astype(F32)
    bias = jnp.concatenate([bias_ref[...]] * rep, axis=0)
    for g in range(kv):
        gl = slice(g * HEAD_DIM, (g + 1) * HEAD_DIM)
        qg = jnp.concatenate([q[:, (g * rep + r) * HEAD_DIM:(g * rep + r + 1) * HEAD_DIM] for r in range(rep)],
                             axis=0).astype(BF16)
        kg = jnp.concatenate([r[0, pl.ds(g, page, stride=kv), :].astype(BF16) for r in kp]
                             + [padded_new(kn_ref[0, :, gl])], axis=0)
        vg = jnp.concatenate([r[0, pl.ds(g, page, stride=kv), :].astype(BF16) for r in vp]
                             + [padded_new(vn_ref[0, :, gl])], axis=0)
        lg = _dot_nt(qg, kg) + bias
        m = jnp.max(lg, axis=1, keepdims=True)
        p = jnp.exp(lg - m)
        den = jnp.sum(p, axis=1, keepdims=True)
        o = _dot(p.astype(BF16), vg) / den
        for r in range(rep):
            hh = g * rep + r
            o_ref[0, :, hh * HEAD_DIM:(hh + 1) * HEAD_DIM] = o[r * t_pad:(r + 1) * t_pad].astype(BF16)


def _dsa_sample(page_table, cache_kidx, cache_k, cache_v, ki16, k16, v16, qir, wis, qn, hb, kv, n_new):
    db, n_pages = page_table.shape
    n_pool, page = cache_kidx.shape[0], cache_kidx.shape[1]
    t_pad = qn.shape[1]
    ck = cache_k.reshape(n_pool, page * kv, HEAD_DIM)
    cv = cache_v.reshape(n_pool, page * kv, HEAD_DIM)
    n_sel = min(TOPK_MAX, (n_pages * page + n_new) // 4)

    def pspec(rows, p):
        return pl.BlockSpec((1, rows, HEAD_DIM), functools.partial(lambda b, pt, p: (pt[b, p], 0, 0), p=p))

    def bspec(width):
        return pl.BlockSpec((1, t_pad, width), lambda b, pt: (b, 0, 0))

    in_specs = ([pspec(page, p) for p in range(n_pages)]
                + [pspec(page * kv, p) for p in range(n_pages)]
                + [pspec(page * kv, p) for p in range(n_pages)]
                + [bspec(IDX_DIM), bspec(kv * HEAD_DIM), bspec(kv * HEAD_DIM),
                   bspec(IDX_HEADS * IDX_DIM), bspec(IDX_HEADS), bspec(hb * HEAD_DIM)])
    s_tot = n_pages * page + LANES
    grid_spec = pltpu.PrefetchScalarGridSpec(
        num_scalar_prefetch=1,
        grid=(db,),
        in_specs=in_specs,
        out_specs=pl.BlockSpec((1, t_pad, hb * HEAD_DIM), lambda b, pt: (b, 0, 0)),
        scratch_shapes=[pltpu.VMEM((t_pad, s_tot), I32), pltpu.VMEM((t_pad, s_tot), F32)],
    )
    return pl.pallas_call(
        functools.partial(_dsa_sample_kernel, n_pages=n_pages, hb=hb, kv=kv, n_sel=n_sel, n_new=n_new),
        grid_spec=grid_spec,
        out_shape=jax.ShapeDtypeStruct((db, t_pad, hb * HEAD_DIM), BF16),
        compiler_params=_cparams(("arbitrary",)),
        name="dsa_sample",
    )(page_table, *([cache_kidx] * n_pages), *([ck] * n_pages), *([cv] * n_pages),
      ki16, k16, v16, qir, wis, qn)


def _resid_router_kernel(x_ref, o_ref, g1_ref, g_ref, sc_ref, sh_ref, wr_ref, br_ref,
                         x1_ref, h_ref, ids_ref, gates_ref):
    x1 = x_ref[...] + g1_ref[...] * o_ref[...]
    x1_ref[...] = x1
    h = _rms_mod(x1, g_ref[...], sc_ref[...], sh_ref[...])
    h_ref[...] = h.astype(BF16)
    bb, tl, d = h.shape
    n_e = wr_ref.shape[1]
    logits = _dot(h.reshape(bb * tl, d), wr_ref[...], precision=lax.Precision.HIGHEST) + br_ref[...]
    lane = lax.broadcasted_iota(I32, logits.shape, 1)
    vals, ids = [], []
    for _ in range(TOP_K):
        m = jnp.max(logits, axis=1, keepdims=True)
        idx = jnp.min(jnp.where(logits == m, lane, n_e), axis=1, keepdims=True)
        vals.append(m)
        ids.append(idx)
        logits = jnp.where(lane == idx, -jnp.inf, logits)
    es = [jnp.exp(v - vals[0]) for v in vals]
    tot = es[0] + es[1] + es[2] + es[3]
    ids_ref[...] = jnp.concatenate(ids, axis=1).reshape(bb, tl, TOP_K)
    gates_ref[...] = jnp.concatenate([e / tot for e in es], axis=1).reshape(bb, tl, TOP_K)


def _resid_router(x, o, g1, g, sc, sh, wr, br, bb, tl):
    nb, l, d = x.shape
    n_e = wr.shape[1]
    xs = pl.BlockSpec((bb, tl, d), lambda b, i: (b, i, 0))
    ms = pl.BlockSpec((bb, 1, d), lambda b, i: (b, 0, 0))
    ks = pl.BlockSpec((bb, tl, TOP_K), lambda b, i: (b, i, 0))
    return pl.pallas_call(
        _resid_router_kernel,
        grid=(nb // bb, l // tl),
        in_specs=[xs, xs, ms, pl.BlockSpec((1, 1, d), lambda b, i: (0, 0, 0)), ms, ms,
                  pl.BlockSpec((d, n_e), lambda b, i: (0, 0)), pl.BlockSpec((1, n_e), lambda b, i: (0, 0))],
        out_specs=[xs, xs, ks, ks],
        out_shape=[jax.ShapeDtypeStruct((nb, l, d), F32), jax.ShapeDtypeStruct((nb, l, d), BF16),
                   jax.ShapeDtypeStruct((nb, l, TOP_K), I32), jax.ShapeDtypeStruct((nb, l, TOP_K), F32)],
        compiler_params=_cparams(("arbitrary", "arbitrary")),
        name="resid_router",
    )(x, o, g1, g.reshape(1, 1, d), sc, sh, wr, br.reshape(1, n_e))


def _row_gather_kernel(n_ref, idx_ref, src_ref, dst_ref, sem, *, chunk):
    valid = pl.program_id(0) * chunk < n_ref[0]

    def copy(r):
        return pltpu.make_async_copy(src_ref.at[idx_ref[0, 0, r]], dst_ref.at[r], sem)

    @pl.when(valid)
    def _():
        def start(r, c):
            copy(r).start()
            return c
        lax.fori_loop(0, chunk, start, 0, unroll=8)

        def wait(r, c):
            copy(r).wait()
            return c
        lax.fori_loop(0, chunk, wait, 0, unroll=8)

    @pl.when(jnp.logical_not(valid))
    def _():
        dst_ref[...] = jnp.zeros_like(dst_ref)


def _row_gather(src3, idx, n_rows, n_out, chunk):
    n_chunks = n_out // chunk
    grid_spec = pltpu.PrefetchScalarGridSpec(
        num_scalar_prefetch=1,
        grid=(n_chunks,),
        in_specs=[pl.BlockSpec((1, 1, chunk), lambda i, n: (i, 0, 0), memory_space=pltpu.SMEM),
                  pl.BlockSpec(memory_space=pl.ANY)],
        out_specs=pl.BlockSpec((chunk,) + src3.shape[1:], lambda i, n: (i, 0, 0)),
        scratch_shapes=[pltpu.SemaphoreType.DMA],
    )
    return pl.pallas_call(
        functools.partial(_row_gather_kernel, chunk=chunk),
        grid_spec=grid_spec,
        out_shape=jax.ShapeDtypeStruct((n_out,) + src3.shape[1:], src3.dtype),
        compiler_params=_cparams(("arbitrary",)),
        name="moe_dispatch",
    )(n_rows.reshape(1), idx.reshape(n_chunks, 1, chunk), src3)


def _gmm_kernel(*refs, swiglu, tn):
    if swiglu:
        e_ref, st_ref, ns_ref, nv_ref, x_ref, w_ref, b_ref, perm_ref, o_hbm, wb_ref, stage, sem, cnt = refs
    else:
        e_ref, st_ref, ns_ref, nv_ref, x_ref, w_ref, b_ref, o_hbm, wb_ref, stage, sem, cnt = refs
    del e_ref
    s, j = pl.program_id(0), pl.program_id(1)
    n = ns_ref[s]

    def out_copy(slot, sub, col):
        dst = o_hbm.at[pl.ds(pl.multiple_of(sub * MOE_SUB, MOE_SUB), MOE_SUB), pl.ds(pl.multiple_of(col, LANES), tn)]
        return pltpu.make_async_copy(stage.at[slot], dst, sem.at[slot])

    @pl.when((s == 0) & (j == 0))
    def _():
        cnt[0] = 0
        cnt[1] = 0

    @pl.when(n > 0)
    def _():
        wb_ref[...] = w_ref[...].astype(BF16)
        bias = b_ref[...]

        def body(r, c):
            slot = r % 2
            rows = pl.ds(pl.multiple_of(r * MOE_SUB, MOE_SUB), MOE_SUB)
            hh = _dot(x_ref[rows, :], wb_ref[...]) + bias
            if swiglu:
                hp = _dot(hh.astype(BF16), perm_ref[...])
                xg = jnp.minimum(hp[:, :tn], SWIGLU_LIMIT)
                xl = jnp.clip(hp[:, tn:], -SWIGLU_LIMIT, SWIGLU_LIMIT)
                hh = xg * _sigmoid(SWIGLU_ALPHA * xg) * (xl + 1.0)

            @pl.when(cnt[slot] > 0)
            def _():
                out_copy(slot, 0, 0).wait()
            stage[slot] = hh.astype(BF16)
            out_copy(slot, st_ref[s] + r, j * tn).start()
            cnt[slot] = 1
            return c
        lax.fori_loop(0, n, body, 0)

    @pl.when((s == pl.num_programs(0) - 1) & (j == pl.num_programs(1) - 1))
    def _():
        for slot in range(2):
            @pl.when(cnt[slot] > 0)
            def _(slot=slot):
                out_copy(slot, 0, 0).wait()
                cnt[slot] = 0
        stage[0] = jnp.zeros(stage.shape[1:], BF16)
        n_cols = pl.num_programs(1)

        def fill(i, c):
            out_copy(0, nv_ref[0] + i // n_cols, (i % n_cols) * tn).start()
            return c

        def drain(i, c):
            out_copy(0, 0, 0).wait()
            return c
        n_fill = (o_hbm.shape[0] // MOE_SUB - nv_ref[0]) * n_cols
        lax.fori_loop(0, n_fill, fill, 0)
        lax.fori_loop(0, n_fill, drain, 0)


def _gmm(x_sorted, w, b, perm, sup_e, sup_st, sup_ns, n_valid_sub, rs, tn):
    p_alloc, k = x_sorted.shape
    n_e, _, n_w = w.shape
    swiglu = perm is not None
    wt = 2 * tn if swiglu else tn
    n_j = n_w // wt
    s_max = sup_e.shape[0]

    def j_eff(s, j, ns):
        return jnp.where(ns[s] > 0, j, n_j - 1)

    in_specs = [pl.BlockSpec((pl.Element(rs), pl.Element(k)), lambda s, j, e, st, ns, nv: (st[s] * MOE_SUB, 0)),
                pl.BlockSpec((None, k, wt), lambda s, j, e, st, ns, nv: (e[s], 0, j_eff(s, j, ns))),
                pl.BlockSpec((None, 1, wt), lambda s, j, e, st, ns, nv: (e[s], 0, j_eff(s, j, ns)))]
    args = [x_sorted, w, b.reshape(n_e, 1, n_w)]
    if swiglu:
        in_specs.append(pl.BlockSpec((wt, wt), lambda s, j, e, st, ns, nv: (0, 0)))
        args.append(perm)
    grid_spec = pltpu.PrefetchScalarGridSpec(
        num_scalar_prefetch=4,
        grid=(s_max, n_j),
        in_specs=in_specs,
        out_specs=pl.BlockSpec(memory_space=pl.ANY),
        scratch_shapes=[pltpu.VMEM((k, wt), BF16), pltpu.VMEM((2, MOE_SUB, tn), BF16),
                        pltpu.SemaphoreType.DMA((2,)), pltpu.SMEM((2,), I32)],
    )
    return pl.pallas_call(
        functools.partial(_gmm_kernel, swiglu=swiglu, tn=tn),
        grid_spec=grid_spec,
        out_shape=jax.ShapeDtypeStruct((p_alloc, n_j * tn), BF16),
        compiler_params=_cparams(("arbitrary", "arbitrary")),
        name="moe_gmm1" if swiglu else "moe_gmm2",
    )(sup_e, sup_st, sup_ns, n_valid_sub.reshape(1), *args)


def _deinterleave_matrix(ft):
    pr = np.arange(2 * ft)
    perm = np.zeros((2 * ft, 2 * ft), np.float32)
    perm[pr, np.where(pr % 2 == 0, pr // 2, ft + pr // 2)] = 1.0
    return jnp.asarray(perm, BF16)


SLAB_PITCH = 40


def _combine_kernel(slot_ref, gate_ref, y_ref, x_ref, g2_ref, o_ref, buf, acc, sem, *, tt, n_slab):
    def copy(r):
        return pltpu.make_async_copy(y_ref.at[slot_ref[0, 0, r]], buf.at[r], sem)

    def start(r, c):
        copy(r).start()
        return c
    lax.fori_loop(0, tt * TOP_K, start, 0, unroll=8)

    def wait(r, c):
        copy(r).wait()
        return c
    lax.fori_loop(0, tt * TOP_K, wait, 0, unroll=8)

    def body(t, c):
        a = gate_ref[0, 0, t * TOP_K] * buf[t * TOP_K].astype(F32)
        for k in range(1, TOP_K):
            a = a + gate_ref[0, 0, t * TOP_K + k] * buf[t * TOP_K + k].astype(F32)
        acc[pl.ds(pl.multiple_of(t * SLAB_PITCH, SUBLANES), n_slab), :] = a
        return c
    lax.fori_loop(0, tt, body, 0)
    for s in range(n_slab):
        ls = slice(s * LANES, (s + 1) * LANES)
        m = acc[pl.ds(s, tt, stride=SLAB_PITCH), :]
        o_ref[:, ls] = x_ref[:, ls] + g2_ref[:, ls] * m


def _combine(y3, slots, gates, x1, g2, tt):
    r_tot, d = x1.shape
    n_slab, ln = y3.shape[1:]
    n_t = r_tot // tt
    if g2.ndim == 3:
        per = r_tot // g2.shape[0] // tt
        g2_spec = pl.BlockSpec((None, 1, d), lambda i: (i // per, 0, 0))
    else:
        g2_spec = pl.BlockSpec((tt, d), lambda i: (i, 0))
    return pl.pallas_call(
        functools.partial(_combine_kernel, tt=tt, n_slab=n_slab),
        grid=(n_t,),
        in_specs=[pl.BlockSpec((1, 1, tt * TOP_K), lambda i: (i, 0, 0), memory_space=pltpu.SMEM),
                  pl.BlockSpec((1, 1, tt * TOP_K), lambda i: (i, 0, 0), memory_space=pltpu.SMEM),
                  pl.BlockSpec(memory_space=pl.ANY),
                  pl.BlockSpec((tt, d), lambda i: (i, 0)),
                  g2_spec],
        out_specs=pl.BlockSpec((tt, d), lambda i: (i, 0)),
        out_shape=jax.ShapeDtypeStruct((r_tot, d), F32),
        scratch_shapes=[pltpu.VMEM((tt * TOP_K, n_slab, ln), y3.dtype),
                        pltpu.VMEM((tt * SLAB_PITCH, ln), F32), pltpu.SemaphoreType.DMA],
        compiler_params=_cparams(("arbitrary",)),
        name="moe_combine",
    )(slots.reshape(n_t, 1, tt * TOP_K), gates.reshape(n_t, 1, tt * TOP_K), y3, x1, g2)


def _routing(ids, n_e, rs):
    t_tot = ids.shape[0]
    n_asg = t_tot * TOP_K
    e = ids.reshape(n_asg)
    onehot = (e[:, None] == jnp.arange(n_e, dtype=I32)[None, :]).astype(I32)
    csum = jnp.cumsum(onehot, axis=0)
    counts = csum[-1]
    rank = jnp.sum(onehot * csum, axis=1) - 1
    pc = ((counts + MOE_SUB - 1) // MOE_SUB) * MOE_SUB
    gs = jnp.cumsum(pc) - pc
    slot = gs[e] + rank
    n_rows = jnp.sum(pc)
    p_max = ((n_asg + n_e * (MOE_SUB - 1)) // MOE_SUB + 1) * MOE_SUB
    p_alloc = p_max + rs
    tok = jnp.zeros((p_alloc,), I32).at[slot].set(jnp.arange(n_asg, dtype=I32) // TOP_K)
    subs = rs // MOE_SUB
    ns_e = pc // MOE_SUB
    nsup_e = (ns_e + subs - 1) // subs
    sup_end = jnp.cumsum(nsup_e)
    total = sup_end[-1]
    s_max = n_e + p_max // rs + 1
    s_idx = jnp.arange(s_max, dtype=I32)
    s_clip = jnp.minimum(s_idx, total - 1)
    e_s = jnp.minimum(jnp.searchsorted(sup_end, s_clip, side="right").astype(I32), n_e - 1)
    j_s = s_clip - (sup_end[e_s] - nsup_e[e_s])
    start = ((gs[e_s] + j_s * rs) // MOE_SUB).astype(I32)
    nsub = jnp.where(s_idx < total, jnp.clip(ns_e[e_s] - j_s * subs, 0, subs), 0).astype(I32)
    return slot.astype(I32), tok, n_rows.astype(I32), p_alloc, e_s, start, nsub


def kernel(x_prompt, x_sample, cache_k, cache_v, cache_kidx, state_hgrn, page_table, c_prompt, c_sample,
           w_ada, b_ada, norm1, norm2, w_in, w_out, hgrn_lb, g_hgrn_o, g_q, g_k, g_kidx,
           w_router, b_router, w1, b1, w2, b2):
    b_sz, seq, d = x_prompt.shape
    db, t_dec, _ = x_sample.shape
    depth = w_ada.shape[0]
    n_pages, page = page_table.shape[1], cache_k.shape[2]
    past_len = n_pages * page
    ha = d // (2 * HEAD_DIM)
    hb = d // (2 * HEAD_DIM)
    kv = hb // 4
    n_e = w_router.shape[2]
    hw = ha * HEAD_DIM
    col_qb = 4 * hw
    col_tail = col_qb + hb * HEAD_DIM + 2 * kv * HEAD_DIM + IDX_HEADS * IDX_DIM
    n_cols = w_in.shape[2]
    assert n_cols == col_tail + IDX_DIM + IDX_HEADS and t_dec <= DEC_PAD

    lb_cum = jnp.cumsum(jax.nn.softmax(hgrn_lb.astype(F32), axis=0), axis=0)
    pos_p = jnp.arange(seq)
    pos_s = past_len + jnp.arange(DEC_PAD)
    tl_p = _pick(seq, 256, 16)
    sb = _pick(db, 16, 1)
    c_prompt_len = _pick(seq, 64, HGRN_SUB)
    rs = MOE_SUB * MOE_SUBS_PER_SUPER

    xp = x_prompt
    xs = jnp.pad(x_sample, ((0, 0), (0, DEC_PAD - t_dec), (0, 0)))
    n_c = b_sz + db
    c_all = jnp.pad(jnp.concatenate([c_prompt, c_sample], axis=0), ((0, (-n_c) % SUBLANES), (0, 0)))
    outs = [[] for _ in range(8)]
    for l in range(depth):
        lb = lb_cum[l + 1] - lb_cum[0]
        mod = _ada(c_all, w_ada[l], b_ada[l])
        mp = [mod[:b_sz, i * d:(i + 1) * d][:, None, :] for i in range(6)]
        ms = [mod[b_sz:n_c, i * d:(i + 1) * d][:, None, :] for i in range(6)]
        w_tail = w_in[l][:, col_tail:]
        res = []
        for (x, m, nb, ln, bb, tl, pos, s0, heads_blk, c_len, n_valid, past) in (
                (xp, mp, b_sz, seq, 1, tl_p, pos_p, None, min(4, ha), c_prompt_len, None, False),
                (xs, ms, db, DEC_PAD, sb, DEC_PAD, pos_s, state_hgrn[l], ha, HGRN_SUB, t_dec, True)):
            h = _norm_mod(x, norm1[l], m[1], m[0], bb, tl)
            h2d = h.reshape(nb * ln, d)
            u = _proj([h2d], w_in[l], col_tail).reshape(nb, ln, col_tail)
            ut = _proj([h2d], w_tail, n_cols - col_tail).reshape(nb, ln, n_cols - col_tail)
            oa, s_new = _hgrn(u, 0, lb, g_hgrn_o[l], s0, ha, heads_blk, c_len, n_valid)
            qn, kn, k16, vo, v16, qir, kio, ki16, wis = _prep(u, ut, col_qb, hb, kv, pos, g_q[l], g_k[l],
                                                              g_kidx[l], bb, tl)
            if past:
                ob = _dsa_sample(page_table, cache_kidx[l], cache_k[l], cache_v[l], ki16, k16, v16,
                                 qir, wis, qn, hb, kv, t_dec)
            else:
                ob = _dsa_prompt(qir, wis, qn, ki16, k16, v16, hb, kv, _pick(ln, 256, 16))
            o = _proj([oa.reshape(nb * ln, hw), ob.reshape(nb * ln, hb * HEAD_DIM)], w_out[l], d)
            x1, h2, ids, gates = _resid_router(x, o.reshape(nb, ln, d), m[2], norm2[l], m[4], m[3],
                                               w_router[l], b_router[l], bb, tl)
            res.append((x1, h2, ids, gates, kn, vo, kio, s_new))
        (x1p, h2p, idp, gtp, knp, vop, kiop, snp), (x1s, h2s, ids_, gts, kns, vos, kios, sns) = res
        t_p, t_s = b_sz * seq, db * t_dec
        t_tot = t_p + t_s
        ids_all = jnp.concatenate([idp.reshape(t_p, TOP_K), ids_[:, :t_dec].reshape(t_s, TOP_K)], axis=0)
        gates_all = jnp.concatenate([gtp.reshape(t_p, TOP_K), gts[:, :t_dec].reshape(t_s, TOP_K)], axis=0)
        h_all = jnp.concatenate([h2p.reshape(t_p, d), h2s[:, :t_dec].reshape(t_s, d)], axis=0)
        slot, tok, n_rows, p_alloc, sup_e, sup_st, sup_ns = _routing(ids_all, n_e, rs)
        x_sorted = _row_gather(h_all.reshape(t_tot, d // LANES, LANES), tok, n_rows, p_alloc, MOE_SUB)
        ft = _pick(w1.shape[3] // 2, 256, LANES)
        n_sub = n_rows // MOE_SUB
        a_sorted = _gmm(x_sorted.reshape(p_alloc, d), w1[l], b1[l], _deinterleave_matrix(ft),
                        sup_e, sup_st, sup_ns, n_sub, rs, ft)
        y_sorted = _gmm(a_sorted, w2[l], b2[l], None, sup_e, sup_st, sup_ns, n_sub, rs, _pick(d, 512, LANES))
        y3 = y_sorted.reshape(p_alloc, d // LANES, LANES)
        gate_f = gates_all.reshape(-1)
        n_ap = t_p * TOP_K
        xp = _combine(y3, slot[:n_ap], gate_f[:n_ap], x1p.reshape(t_p, d), mp[5],
                      _pick(seq, 64, SUBLANES)).reshape(b_sz, seq, d)
        g2_tok = jnp.broadcast_to(ms[5], (db, t_dec, d)).reshape(t_s, d)
        xs_valid = _combine(y3, slot[n_ap:], gate_f[n_ap:], x1s[:, :t_dec].reshape(t_s, d), g2_tok,
                            _pick(t_s, 64, SUBLANES)).reshape(db, t_dec, d)
        xs = jnp.pad(xs_valid, ((0, 0), (0, DEC_PAD - t_dec), (0, 0)))
        for lst, v in zip(outs, (knp.reshape(b_sz, seq, kv, HEAD_DIM), vop.reshape(b_sz, seq, kv, HEAD_DIM), kiop,
                                 snp, kns[:, :t_dec].reshape(db, t_dec, kv, HEAD_DIM),
                                 vos[:, :t_dec].reshape(db, t_dec, kv, HEAD_DIM), kios[:, :t_dec], sns)):
            lst.append(v)
    return (xp, xs[:, :t_dec]) + tuple(jnp.stack(o) for o in outs)
```
